```python
import math
import jax, jax.numpy as jnp
from jax import lax
import numpy as np

D_MODEL = 4096
BATCH = 4
SEQ = 4096
DEPTH = 1
DEC_BATCH = 2
DEC_SEQ = 8192
PAST_LEN = 128

D_MIX = D_MODEL
D_ATTN = D_MIX // 2
D_CONV = D_MIX - D_ATTN
N_DIFF_HEADS = 8
DIFF_HEAD_DIM = D_ATTN // N_DIFF_HEADS // 2
ROT_DIM = DIFF_HEAD_DIM // 4
ROPE_THETA = 500000.0
Q_BLOCK = 128
CONV_SIZE = 31
CONV_PAD = (CONV_SIZE - 1) // 2
N_CONV_GROUPS = 16
Q_END = 2 * N_DIFF_HEADS * DIFF_HEAD_DIM
K_END = Q_END + 2 * N_DIFF_HEADS * DIFF_HEAD_DIM
V_END = K_END + D_ATTN
N_IN = V_END + 2 * D_CONV
N_EXPERTS = 32
TOP_K = 4
D_FF = D_MODEL
SWIGLU_LIMIT = 7.0
SWIGLU_ALPHA = 1.702
EXPERT_BLOCK = 128
LN_EPS = 1e-5
DEEPNORM_ALPHA = (2.0 * DEPTH) ** 0.25
DEEPNORM_BETA = (8.0 * DEPTH) ** -0.25

kernel_name = 'hybrid_diffattn_conformer_moe_encoder'


def layer_norm(x, g, b):
    xf = x.astype(jnp.float32)
    mu = jnp.mean(xf, axis=-1, keepdims=True)
    var = jnp.mean(jnp.square(xf - mu), axis=-1, keepdims=True)
    y = (xf - mu) * lax.rsqrt(var + LN_EPS) * g.astype(jnp.float32) + b.astype(jnp.float32)
    return y.astype(x.dtype)


def rms_norm(x, g):
    xf = x.astype(jnp.float32)
    y = xf * lax.rsqrt(jnp.mean(jnp.square(xf), axis=-1, keepdims=True) + LN_EPS) * g.astype(jnp.float32)
    return y.astype(x.dtype)


def rope_tables(seq_len):
    inv_freq = ROPE_THETA ** (-jnp.arange(0, ROT_DIM, 2, dtype=jnp.float32) / ROT_DIM)
    ang = jnp.arange(seq_len, dtype=jnp.float32)[:, None] * inv_freq[None, :]
    return jnp.cos(ang), jnp.sin(ang)


def partial_rope(t, cos, sin):
    half = ROT_DIM // 2
    c = cos[None, :, None, None, :].astype(t.dtype)
    s = sin[None, :, None, None, :].astype(t.dtype)
    r1, r2, rest = t[..., :half], t[..., half:ROT_DIM], t[..., ROT_DIM:]
    return jnp.concatenate([r1 * c - r2 * s, r2 * c + r1 * s, rest], axis=-1)


def diff_attention(q, k, v, lam):
    B, S = q.shape[0], q.shape[1]
    nq = S // Q_BLOCK
    scale = DIFF_HEAD_DIM ** -0.5
    qb = q.reshape(B, nq, Q_BLOCK, N_DIFF_HEADS, 2, DIFF_HEAD_DIM).transpose(1, 0, 2, 3, 4, 5)

    def one_block(qblk):
        s = jnp.einsum('bqhcd,bkhcd->bhcqk', qblk, k, preferred_element_type=jnp.float32) * scale
        p = jax.nn.softmax(s, axis=-1)
        a = p[:, :, 0] - lam * p[:, :, 1]
        return jnp.einsum('bhqk,bkhe->bqhe', a.astype(v.dtype), v)

    out = lax.map(one_block, qb)
    return out.transpose(1, 0, 2, 3, 4).reshape(B, S, N_DIFF_HEADS, 2 * DIFF_HEAD_DIM)


def conformer_conv(c, b_conv_in, conv_w, conv_b, conv_ln_g, conv_ln_b):
    c = c + b_conv_in
    a, g = jnp.split(c, 2, axis=-1)
    u = a * jax.nn.sigmoid(g)
    u = lax.conv_general_dilated(u, conv_w[:, None, :], window_strides=(1,),
                                 padding=[(CONV_PAD, CONV_PAD)],
                                 dimension_numbers=('NWC', 'WIO', 'NWC'),
                                 feature_group_count=D_CONV) + conv_b
    return jax.nn.silu(layer_norm(u, conv_ln_g, conv_ln_b))


def hybrid_mixer(x, cos, sin, w_in, b_conv_in, conv_w, conv_b, conv_ln_g, conv_ln_b,
                 lq1, lk1, lq2, lk2, subln_g, w_out, lambda_init):
    B, S, _ = x.shape
    proj = jnp.einsum('bsd,dn->bsn', x, w_in)
    q, k, v, c = jnp.split(proj, [Q_END, K_END, V_END], axis=-1)
    q = partial_rope(q.reshape(B, S, N_DIFF_HEADS, 2, DIFF_HEAD_DIM), cos, sin)
    k = partial_rope(k.reshape(B, S, N_DIFF_HEADS, 2, DIFF_HEAD_DIM), cos, sin)
    v = v.reshape(B, S, N_DIFF_HEADS, 2 * DIFF_HEAD_DIM)
    lam = (jnp.exp(jnp.sum(lq1.astype(jnp.float32) * lk1.astype(jnp.float32)))
           - jnp.exp(jnp.sum(lq2.astype(jnp.float32) * lk2.astype(jnp.float32))) + lambda_init)
    o = diff_attention(q, k, v, lam)
    o = (rms_norm(o, subln_g) * (1.0 - lambda_init)).reshape(B, S, D_ATTN)
    u = conformer_conv(c, b_conv_in, conv_w, conv_b, conv_ln_g, conv_ln_b)
    mixed = jnp.concatenate([o, u.astype(o.dtype)], axis=-1)
    return jnp.einsum('bsm,md->bsd', mixed, w_out)


def moe_ffn(x2d, router_w, router_b, w_gu, b_gu, w_dn, b_dn):
    T = x2d.shape[0]
    logits = jnp.einsum('td,de->te', x2d, router_w, preferred_element_type=jnp.float32) + router_b.astype(jnp.float32)
    top_val, top_idx = lax.top_k(logits, TOP_K)
    gates = jax.nn.softmax(top_val, axis=-1)
    A = T * TOP_K
    flat_e = top_idx.reshape(-1).astype(jnp.int32)
    flat_tok = jnp.arange(A, dtype=jnp.int32) // TOP_K
    flat_gate = gates.reshape(-1)
    order = jnp.argsort(flat_e)
    sorted_e = flat_e[order]
    counts = jnp.bincount(flat_e, length=N_EXPERTS).astype(jnp.int32)
    padded = ((counts + EXPERT_BLOCK - 1) // EXPERT_BLOCK) * EXPERT_BLOCK
    padded_end = jnp.cumsum(padded)
    padded_start = padded_end - padded
    start = jnp.cumsum(counts) - counts
    rank = jnp.arange(A, dtype=jnp.int32) - start[sorted_e]
    dest = padded_start[sorted_e] + rank
    NB = -(-A // EXPERT_BLOCK) + N_EXPERTS
    P = NB * EXPERT_BLOCK
    slot_tok = jnp.full((P,), T, jnp.int32).at[dest].set(flat_tok[order])
    slot_gate = jnp.zeros((P,), jnp.float32).at[dest].set(flat_gate[order])
    block_e = jnp.clip(jnp.searchsorted(padded_end, jnp.arange(NB, dtype=jnp.int32) * EXPERT_BLOCK, side='right'),
                       0, N_EXPERTS - 1).astype(jnp.int32)
    x_pad = jnp.concatenate([x2d, jnp.zeros((1, x2d.shape[1]), x2d.dtype)], axis=0)

    def expert_block(args):
        e, tok, gate = args
        xb = x_pad[tok]
        hgu = xb @ w_gu[e] + b_gu[e]
        g, up = jnp.split(hgu, 2, axis=-1)
        g = jnp.minimum(g, SWIGLU_LIMIT)
        up = jnp.clip(up, -SWIGLU_LIMIT, SWIGLU_LIMIT)
        h = (up + 1.0) * (g * jax.nn.sigmoid(g * SWIGLU_ALPHA))
        out = h @ w_dn[e] + b_dn[e]
        return out * gate[:, None].astype(out.dtype)

    outs = lax.map(expert_block, (block_e, slot_tok.reshape(NB, EXPERT_BLOCK), slot_gate.reshape(NB, EXPERT_BLOCK)))
    y = jnp.zeros((T + 1, x2d.shape[1]), outs.dtype).at[slot_tok].add(outs.reshape(P, -1))
    return y[:T].astype(x2d.dtype)


def encoder_trunk(x, w_in, b_conv_in, conv_w, conv_b, conv_ln_g, conv_ln_b,
                  lambda_q1, lambda_k1, lambda_q2, lambda_k2, subln_g, w_out, ln1_g, ln1_b,
                  router_w, router_b, w_gate_up, b_gate_up, w_down, b_down, ln2_g, ln2_b):
    B, S, D = x.shape
    cos, sin = rope_tables(S)
    for l in range(DEPTH):
        lambda_init = 0.8 - 0.6 * math.exp(-0.3 * l)
        m = hybrid_mixer(x, cos, sin, w_in[l], b_conv_in[l], conv_w[l], conv_b[l], conv_ln_g[l], conv_ln_b[l],
                         lambda_q1[l], lambda_k1[l], lambda_q2[l], lambda_k2[l], subln_g[l], w_out[l], lambda_init)
        x = layer_norm(DEEPNORM_ALPHA * x + m, ln1_g[l], ln1_b[l])
        f = moe_ffn(x.reshape(B * S, D), router_w[l], router_b[l], w_gate_up[l], b_gate_up[l],
                    w_down[l], b_down[l]).reshape(B, S, D)
        x = layer_norm(DEEPNORM_ALPHA * x + f, ln2_g[l], ln2_b[l])
    return x


def setup_inputs(seed: int = 0) -> dict:
    key = jax.random.key(seed)
    ks = jax.random.split(key, 24)

    def nrm(k, shape, scale):
        return jax.random.normal(k, shape, jnp.float32) * scale

    return {
        'x_prompt': nrm(ks[0], (BATCH, SEQ, D_MODEL), 1.0),
        'x_sample': nrm(ks[1], (DEC_BATCH, DEC_SEQ, D_MODEL), 1.0),
        'w_in': nrm(ks[2], (DEPTH, D_MODEL, N_IN), D_MODEL ** -0.5),
        'b_conv_in': nrm(ks[3], (DEPTH, 2 * D_CONV), 0.02),
        'conv_w': nrm(ks[4], (DEPTH, CONV_SIZE, D_CONV), CONV_SIZE ** -0.5),
        'conv_b': nrm(ks[5], (DEPTH, D_CONV), 0.02),
        'conv_ln_g': 1.0 + nrm(ks[6], (DEPTH, D_CONV), 0.02),
        'conv_ln_b': nrm(ks[7], (DEPTH, D_CONV), 0.02),
        'lambda_q1': nrm(ks[8], (DEPTH, DIFF_HEAD_DIM), 0.1),
        'lambda_k1': nrm(ks[9], (DEPTH, DIFF_HEAD_DIM), 0.1),
        'lambda_q2': nrm(ks[10], (DEPTH, DIFF_HEAD_DIM), 0.1),
        'lambda_k2': nrm(ks[11], (DEPTH, DIFF_HEAD_DIM), 0.1),
        'subln_g': 1.0 + nrm(ks[12], (DEPTH, 2 * DIFF_HEAD_DIM), 0.02),
        'w_out': nrm(ks[13], (DEPTH, D_MIX, D_MODEL), D_MIX ** -0.5 * DEEPNORM_BETA),
        'ln1_g': 1.0 + nrm(ks[14], (DEPTH, D_MODEL), 0.02),
        'ln1_b': nrm(ks[15], (DEPTH, D_MODEL), 0.02),
        'router_w': nrm(ks[16], (DEPTH, D_MODEL, N_EXPERTS), D_MODEL ** -0.5),
        'router_b': nrm(ks[17], (DEPTH, N_EXPERTS), 0.01),
        'w_gate_up': nrm(ks[18], (DEPTH, N_EXPERTS, D_MODEL, 2 * D_FF), D_MODEL ** -0.5),
        'b_gate_up': nrm(ks[19], (DEPTH, N_EXPERTS, 2 * D_FF), 0.02),
        'w_down': nrm(ks[20], (DEPTH, N_EXPERTS, D_FF, D_MODEL), D_FF ** -0.5 * DEEPNORM_BETA),
        'b_down': nrm(ks[21], (DEPTH, N_EXPERTS, D_MODEL), 0.02),
        'ln2_g': 1.0 + nrm(ks[22], (DEPTH, D_MODEL), 0.02),
        'ln2_b': nrm(ks[23], (DEPTH, D_MODEL), 0.02),
    }


def reference(x_prompt, x_sample, w_in, b_conv_in, conv_w, conv_b, conv_ln_g, conv_ln_b,
              lambda_q1, lambda_k1, lambda_q2, lambda_k2, subln_g, w_out, ln1_g, ln1_b,
              router_w, router_b, w_gate_up, b_gate_up, w_down, b_down, ln2_g, ln2_b):
    y_prompt = encoder_trunk(x_prompt, w_in, b_conv_in, conv_w, conv_b, conv_ln_g, conv_ln_b,
                             lambda_q1, lambda_k1, lambda_q2, lambda_k2, subln_g, w_out, ln1_g, ln1_b,
                             router_w, router_b, w_gate_up, b_gate_up, w_down, b_down, ln2_g, ln2_b)
    y_sample = encoder_trunk(x_sample, w_in, b_conv_in, conv_w, conv_b, conv_ln_g, conv_ln_b,
                             lambda_q1, lambda_k1, lambda_q2, lambda_k2, subln_g, w_out, ln1_g, ln1_b,
                             router_w, router_b, w_gate_up, b_gate_up, w_down, b_down, ln2_g, ln2_b)
    return (y_prompt, y_sample)
```

```python
import functools
import math

import jax
import jax.numpy as jnp
from jax import lax
from jax.experimental import pallas as pl
from jax.experimental.pallas import tpu as pltpu

F32 = jnp.float32
BF16 = jnp.bfloat16
I32 = jnp.int32

LANES = 128
HALO = 16
HEAD_DIM = 128
HEAD_W = 2 * HEAD_DIM
ROT_DIM = HEAD_DIM // 4
ROT_HALF = ROT_DIM // 2
ROPE_THETA = 500000.0
CONV_SIZE = 31
CONV_PAD = (CONV_SIZE - 1) // 2
TOP_K = 4
SWIGLU_LIMIT = 7.0
SWIGLU_ALPHA = 1.702
LN_EPS = 1e-5
LOG2E = 1.4426950408889634
VMEM_LIMIT = 56 * 1024 * 1024


def _tile(dim, pref):
    t = pref
    while t > 1 and dim % t:
        t //= 2
    return t


def _params(sem, vmem=VMEM_LIMIT):
    return pltpu.CompilerParams(dimension_semantics=sem, vmem_limit_bytes=vmem)


def _layer_norm(z, g, b):
    mu = jnp.mean(z, axis=-1, keepdims=True)
    zc = z - mu
    var = jnp.mean(zc * zc, axis=-1, keepdims=True)
    return zc * lax.rsqrt(var + LN_EPS) * g + b


def _row_copy(i, n_p, tm, xp_hbm, xs_hbm, dst, sem):
    cp_p = pltpu.make_async_copy(xp_hbm.at[pl.ds(jnp.minimum(i, n_p - 1) * tm, tm), :], dst, sem)
    cp_s = pltpu.make_async_copy(xs_hbm.at[pl.ds(jnp.maximum(i - n_p, 0) * tm, tm), :], dst, sem)
    return cp_p, cp_s


def _row_start(i, n_p, tm, xp_hbm, xs_hbm, dst, sem):
    cp_p, cp_s = _row_copy(i, n_p, tm, xp_hbm, xs_hbm, dst, sem)
    pl.when(i < n_p)(cp_p.start)
    pl.when(i >= n_p)(cp_s.start)


def _row_wait(i, n_p, tm, xp_hbm, xs_hbm, dst, sem):
    cp_p, cp_s = _row_copy(i, n_p, tm, xp_hbm, xs_hbm, dst, sem)
    pl.when(i < n_p)(cp_p.wait)
    pl.when(i >= n_p)(cp_s.wait)


def _inproj_kernel(mode, n_p, tm, tn, xp_hbm, xs_hbm, *refs):
    if mode == "qk":
        w_ref, cos_ref, sin_ref, o_ref, xf_ref, xb_ref, sem = refs
    elif mode == "v":
        w_ref, o_ref, xf_ref, xb_ref, sem = refs
    else:
        wa_ref, wg_ref, ba_ref, bg_ref, o_ref, xf_ref, xb_ref, sem = refs
    i = pl.program_id(0)
    j = pl.program_id(1)

    @pl.when(j == 0)
    def _load_rows():
        _row_start(i, n_p, tm, xp_hbm, xs_hbm, xf_ref, sem)
        _row_wait(i, n_p, tm, xp_hbm, xs_hbm, xf_ref, sem)
        xb_ref[...] = xf_ref[...].astype(BF16)

    xb = xb_ref[...]
    if mode == "qk":
        acc = jnp.dot(xb, w_ref[...], preferred_element_type=F32)
        c = cos_ref[...]
        s = sin_ref[...]
        lane = lax.broadcasted_iota(I32, (tm, LANES), 1)
        for ch in range(tn // LANES):
            t = acc[:, ch * LANES:(ch + 1) * LANES]
            partner = jnp.where(lane < ROT_HALF, pltpu.roll(t, LANES - ROT_HALF, 1), pltpu.roll(t, ROT_HALF, 1))
            o_ref[:, ch * LANES:(ch + 1) * LANES] = (t * c + partner * s).astype(BF16)
    elif mode == "v":
        o_ref[...] = jnp.dot(xb, w_ref[...], preferred_element_type=F32).astype(BF16)
    else:
        a = jnp.dot(xb, wa_ref[...], preferred_element_type=F32) + ba_ref[...]
        g = jnp.dot(xb, wg_ref[...], preferred_element_type=F32) + bg_ref[...]
        o_ref[...] = a * (1.0 / (1.0 + jnp.exp(-g)))


def _inproj(mode, xp, xs, w_bf, col0, ncols, tm, tn, extra, out_dtype):
    tp, d = xp.shape
    t_all = tp + xs.shape[0]
    n_p = tp // tm
    cb0 = col0 // tn
    any_spec = pl.BlockSpec(memory_space=pl.ANY)
    w_spec = pl.BlockSpec((d, tn), lambda i, j: (0, cb0 + j))
    if mode == "qk":
        cos_t, sin_t, sp_t, ss_t = extra
        pos = lambda i, j: (jnp.where(i < n_p, i % sp_t, (i - n_p) % ss_t), 0)
        in_specs = [any_spec, any_spec, w_spec, pl.BlockSpec((tm, LANES), pos), pl.BlockSpec((tm, LANES), pos)]
        args = (xp, xs, w_bf, cos_t, sin_t)
    elif mode == "v":
        in_specs = [any_spec, any_spec, w_spec]
        args = (xp, xs, w_bf)
    else:
        bias = extra
        gb0 = (col0 + ncols) // tn
        in_specs = [any_spec, any_spec, w_spec, pl.BlockSpec((d, tn), lambda i, j: (0, gb0 + j)),
                    pl.BlockSpec((1, tn), lambda i, j: (0, j)),
                    pl.BlockSpec((1, tn), lambda i, j: (0, ncols // tn + j))]
        args = (xp, xs, w_bf, w_bf, bias, bias)
    return pl.pallas_call(
        functools.partial(_inproj_kernel, mode, n_p, tm, tn),
        out_shape=jax.ShapeDtypeStruct((t_all, ncols), out_dtype),
        grid=(t_all // tm, ncols // tn),
        in_specs=in_specs,
        out_specs=pl.BlockSpec((tm, tn), lambda i, j: (i, j)),
        scratch_shapes=[pltpu.VMEM((tm, d), F32), pltpu.VMEM((tm, d), BF16), pltpu.SemaphoreType.DMA(())],
        compiler_params=_params(("arbitrary", "arbitrary")),
        name="inproj_" + mode,
    )(*args)


def _attn_kernel(lam_init, q_ref, k_ref, v_ref, lq1_ref, lk1_ref, lq2_ref, lk2_ref, g_ref, o_ref):
    c = (HEAD_DIM ** -0.5) * LOG2E

    def probs(lo):
        s = lax.dot_general(q_ref[:, lo:lo + HEAD_DIM], k_ref[:, lo:lo + HEAD_DIM],
                            (((1,), (1,)), ((), ())), preferred_element_type=F32)
        m = jnp.max(s, axis=-1, keepdims=True)
        p = jnp.exp2((s - m) * c)
        return p, jnp.sum(p, axis=-1, keepdims=True)

    p0, l0 = probs(0)
    p1, l1 = probs(HEAD_DIM)
    lam = (jnp.exp(jnp.sum(lq1_ref[...] * lk1_ref[...], axis=-1, keepdims=True))
           - jnp.exp(jnp.sum(lq2_ref[...] * lk2_ref[...], axis=-1, keepdims=True)) + lam_init)
    a = (p0 * (1.0 / l0) - p1 * (lam / l1)).astype(BF16)
    o = jnp.dot(a, v_ref[...], preferred_element_type=F32)
    ms = jnp.mean(o * o, axis=-1, keepdims=True)
    o_ref[...] = (o * lax.rsqrt(ms + LN_EPS) * g_ref[...] * (1.0 - lam_init)).astype(BF16)


def _attention(qk, v, lam_refs, subln_g, row0, batch, seq, n_heads, lam_init, tq):
    assert row0 % seq == 0 and seq % tq == 0
    qt = seq // tq
    kcol0 = n_heads
    small = pl.BlockSpec((1, HEAD_DIM), lambda b, h, i: (0, 0))
    return pl.pallas_call(
        functools.partial(_attn_kernel, lam_init),
        out_shape=jax.ShapeDtypeStruct((batch * seq, n_heads * HEAD_W), BF16),
        grid=(batch, n_heads, qt),
        in_specs=[pl.BlockSpec((tq, HEAD_W), lambda b, h, i: (row0 // tq + b * qt + i, h)),
                  pl.BlockSpec((seq, HEAD_W), lambda b, h, i: (row0 // seq + b, kcol0 + h)),
                  pl.BlockSpec((seq, HEAD_W), lambda b, h, i: (row0 // seq + b, h)),
                  small, small, small, small,
                  pl.BlockSpec((1, HEAD_W), lambda b, h, i: (0, 0))],
        out_specs=pl.BlockSpec((tq, HEAD_W), lambda b, h, i: (b * qt + i, h)),
        compiler_params=_params(("arbitrary", "arbitrary", "arbitrary")),
        name="diff_attention",
    )(qk, qk, v, *lam_refs, subln_g)


def _conv_kernel(ts, nt, rc, prev_ref, x_ref, next_ref, w_ref, cb_ref, g_ref, b_ref, o_ref, pad_ref, u_ref):
    t = pl.program_id(1)
    c_all = x_ref.shape[1]
    pad_ref[0:HALO, :] = jnp.where(t > 0, prev_ref[...], 0.0)
    pad_ref[HALO:HALO + ts, :] = x_ref[...]
    pad_ref[HALO + ts:, :] = jnp.where(t < nt - 1, next_ref[...], 0.0)
    off = HALO - CONV_PAD
    for cc in range(c_all // LANES):
        cs = slice(cc * LANES, (cc + 1) * LANES)
        for r in range(ts // rc):
            acc = jnp.zeros((rc, LANES), F32)
            for k in range(CONV_SIZE):
                acc = acc + pad_ref[r * rc + k + off:r * rc + k + off + rc, cs] * w_ref[k:k + 1, cs]
            u_ref[r * rc:(r + 1) * rc, cs] = acc
    u = _layer_norm(u_ref[...] + cb_ref[...], g_ref[...], b_ref[...])
    o_ref[...] = (u * (1.0 / (1.0 + jnp.exp(-u)))).astype(BF16)


def _conv(u, conv_w, conv_b, ln_g, ln_b, row0, batch, seq, ts):
    c = u.shape[1]
    nt = seq // ts
    assert row0 % ts == 0 and seq % ts == 0 and ts % HALO == 0
    hb = ts // HALO
    last_h = u.shape[0] // HALO - 1
    base = lambda b, t: row0 // ts + b * nt + t
    vec = pl.BlockSpec((1, c), lambda b, t: (0, 0))
    return pl.pallas_call(
        functools.partial(_conv_kernel, ts, nt, _tile(ts, 32)),
        out_shape=jax.ShapeDtypeStruct((batch * seq, c), BF16),
        grid=(batch, nt),
        in_specs=[pl.BlockSpec((HALO, c), lambda b, t: (jnp.maximum(base(b, t) * hb - 1, 0), 0)),
                  pl.BlockSpec((ts, c), lambda b, t: (base(b, t), 0)),
                  pl.BlockSpec((HALO, c), lambda b, t: (jnp.minimum((base(b, t) + 1) * hb, last_h), 0)),
                  pl.BlockSpec((CONV_SIZE, c), lambda b, t: (0, 0)),
                  vec, vec, vec],
        out_specs=pl.BlockSpec((ts, c), lambda b, t: (b * nt + t, 0)),
        scratch_shapes=[pltpu.VMEM((ts + 2 * HALO, c), F32), pltpu.VMEM((ts, c), F32)],
        compiler_params=_params(("arbitrary", "arbitrary")),
        name="conformer_conv",
    )(u, u, u, conv_w, conv_b, ln_g, ln_b)


def _outproj_kernel(alpha, n_exp, n_p, tm, nk, rc, *refs):
    (op_ref, os_ref, up_ref, us_ref, w_ref, xp_hbm, xs_hbm, g_ref, b_ref, rw_ref, rb_ref,
     h_ref, idx_ref, gate_ref, rank_ref, cnt_ref, acc_ref, x_ref, logit_ref, run_ref, sem) = refs
    i = pl.program_id(0)
    k = pl.program_id(1)
    half = nk // 2

    @pl.when(k == 0)
    def _init():
        acc_ref[...] = jnp.zeros_like(acc_ref)
        _row_start(i, n_p, tm, xp_hbm, xs_hbm, x_ref, sem)

    @pl.when((i == 0) & (k == 0))
    def _init_counts():
        run_ref[...] = jnp.zeros_like(run_ref)

    for src, (lo, is_p) in zip((op_ref, up_ref, os_ref, us_ref), ((0, True), (half, True), (0, False), (half, False))):
        rows = (i < n_p) if is_p else (i >= n_p)

        @pl.when(rows & (k >= lo) & (k < lo + half))
        def _acc(src=src):
            acc_ref[...] += jnp.dot(src[...], w_ref[...], preferred_element_type=F32)

    @pl.when(k == nk - 1)
    def _epilogue():
        _row_wait(i, n_p, tm, xp_hbm, xs_hbm, x_ref, sem)
        for r0 in range(0, tm, rc):
            h = _layer_norm(alpha * x_ref[r0:r0 + rc, :] + acc_ref[r0:r0 + rc, :], g_ref[...], b_ref[...])
            h_ref[r0:r0 + rc, :] = h
            logit_ref[r0:r0 + rc, :] = jnp.dot(h, rw_ref[...], preferred_element_type=F32,
                                               precision=lax.Precision.HIGHEST)
        lane = lax.broadcasted_iota(I32, (tm, LANES), 1)
        lane_f = lane.astype(F32)
        logits = jnp.where(lane < n_exp, logit_ref[...] + rb_ref[...], -jnp.inf)
        idx_out = jnp.zeros((tm, LANES), F32)
        val_out = jnp.zeros((tm, LANES), F32)
        member = jnp.zeros((tm, LANES), F32)
        picks = []
        top = None
        for j in range(TOP_K):
            mx = jnp.max(logits, axis=-1, keepdims=True)
            top = mx if top is None else top
            pick = jnp.min(jnp.where(logits == mx, lane_f, float(LANES)), axis=-1, keepdims=True)
            hit = lane_f == pick
            picks.append(hit)
            idx_out = jnp.where(lane == j, pick, idx_out)
            val_out = jnp.where(lane == j, mx, val_out)
            member = jnp.where(hit, 1.0, member)
            logits = jnp.where(hit, -jnp.inf, logits)
        e = jnp.where(lane < TOP_K, jnp.exp(val_out - top), 0.0)
        gate_ref[...] = e * (1.0 / jnp.sum(e, axis=-1, keepdims=True))
        idx_ref[...] = idx_out.astype(I32)
        row = lax.broadcasted_iota(I32, (tm, tm), 0)
        col = lax.broadcasted_iota(I32, (tm, tm), 1)
        before = jnp.where(col < row, 1.0, 0.0).astype(BF16)
        prior = jnp.dot(before, member.astype(BF16), preferred_element_type=F32) + run_ref[...]
        rank_out = jnp.zeros((tm, LANES), F32)
        for j in range(TOP_K):
            rj = jnp.sum(jnp.where(picks[j], prior, 0.0), axis=-1, keepdims=True)
            rank_out = jnp.where(lane == j, rj, rank_out)
        rank_ref[...] = rank_out.astype(I32)
        run_ref[...] += jnp.sum(member, axis=0, keepdims=True)
        cnt_ref[...] = run_ref[...].astype(I32)


def _outproj(o_p, o_s, u_p, u_s, w_out_bf, xp, xs, ln_g, ln_b, rw_pad, rb_pad, n_exp, alpha, tm, tk):
    tp, d = xp.shape
    t_all = tp + xs.shape[0]
    n_p = tp // tm
    n_s = xs.shape[0] // tm
    half = o_p.shape[1] // tk
    nk = 2 * half
    rowp = lambda i: jnp.minimum(i, n_p - 1)
    rows = lambda i: jnp.clip(i - n_p, 0, n_s - 1)
    attn_k = lambda k: jnp.minimum(k, half - 1)
    conv_k = lambda k: jnp.clip(k - half, 0, half - 1)
    any_spec = pl.BlockSpec(memory_space=pl.ANY)
    vec = pl.BlockSpec((1, d), lambda i, k: (0, 0))
    tok = pl.BlockSpec((tm, LANES), lambda i, k: (i, 0))
    return pl.pallas_call(
        functools.partial(_outproj_kernel, alpha, n_exp, n_p, tm, nk, _tile(tm, 128)),
        out_shape=(jax.ShapeDtypeStruct((t_all, d), F32),
                   jax.ShapeDtypeStruct((t_all, LANES), I32),
                   jax.ShapeDtypeStruct((t_all, LANES), F32),
                   jax.ShapeDtypeStruct((t_all, LANES), I32),
                   jax.ShapeDtypeStruct((1, LANES), I32)),
        grid=(t_all // tm, nk),
        in_specs=[pl.BlockSpec((tm, tk), lambda i, k: (rowp(i), attn_k(k))),
                  pl.BlockSpec((tm, tk), lambda i, k: (rows(i), attn_k(k))),
                  pl.BlockSpec((tm, tk), lambda i, k: (rowp(i), conv_k(k))),
                  pl.BlockSpec((tm, tk), lambda i, k: (rows(i), conv_k(k))),
                  pl.BlockSpec((tk, d), lambda i, k: (k, 0)),
                  any_spec, any_spec, vec, vec,
                  pl.BlockSpec((d, LANES), lambda i, k: (0, 0)),
                  pl.BlockSpec((1, LANES), lambda i, k: (0, 0))],
        out_specs=(pl.BlockSpec((tm, d), lambda i, k: (i, 0)), tok, tok, tok,
                   pl.BlockSpec((1, LANES), lambda i, k: (0, 0))),
        scratch_shapes=[pltpu.VMEM((tm, d), F32), pltpu.VMEM((tm, d), F32), pltpu.VMEM((tm, LANES), F32),
                        pltpu.VMEM((1, LANES), F32),
                        pltpu.SemaphoreType.DMA(())],
        compiler_params=_params(("arbitrary", "arbitrary")),
        name="outproj_ln_router",
    )(o_p, o_s, u_p, u_s, w_out_bf, xp, xs, ln_g, ln_b, rw_pad, rb_pad)


def _gather_rows(tok_ref, n, src_hbm, dst_ref, sem):
    def copy(r):
        return pltpu.make_async_copy(src_hbm.at[pl.ds(tok_ref[r], 1), :], dst_ref.at[pl.ds(r, 1), :], sem)

    def start(r, carry):
        copy(r).start()
        return carry

    def wait(r, carry):
        copy(r).wait()
        return carry

    lax.fori_loop(0, n, start, 0, unroll=8)
    lax.fori_loop(0, n, wait, 0, unroll=8)


def _gate_up_kernel(bm, be_ref, act_ref, tok_ref, h_hbm, wg_ref, wu_ref, bg_ref, bu_ref, o_ref, xf_ref, xb_ref, sem):
    m = pl.program_id(0)
    j = pl.program_id(1)
    active = act_ref[m] > 0

    @pl.when(active & (j == 0))
    def _gather():
        _gather_rows(tok_ref.at[0, 0], bm, h_hbm, xf_ref, sem)
        xb_ref[...] = xf_ref[...].astype(BF16)

    @pl.when(active)
    def _compute():
        xb = xb_ref[...]
        g = jnp.dot(xb, wg_ref[...], preferred_element_type=F32) + bg_ref[...]
        up = jnp.dot(xb, wu_ref[...], preferred_element_type=F32) + bu_ref[...]
        g = jnp.minimum(g, SWIGLU_LIMIT)
        up = jnp.clip(up, -SWIGLU_LIMIT, SWIGLU_LIMIT)
        o_ref[...] = ((up + 1.0) * (g * (1.0 / (1.0 + jnp.exp(-SWIGLU_ALPHA * g))))).astype(BF16)

    @pl.when(jnp.logical_not(active))
    def _idle():
        o_ref[...] = jnp.zeros_like(o_ref)


def _gate_up(block_e, active, slot_tok, h, w_gu_bf, b_gu, bm, tn):
    nb = block_e.shape[0]
    d = h.shape[1]
    f = w_gu_bf.shape[2] // 2
    nj = f // tn
    col = lambda j, act, m: jnp.where(act[m] > 0, j, nj - 1)
    grid_spec = pltpu.PrefetchScalarGridSpec(
        num_scalar_prefetch=2,
        grid=(nb, nj),
        in_specs=[pl.BlockSpec((1, 1, bm), lambda m, j, be, act: (m, 0, 0), memory_space=pltpu.SMEM),
                  pl.BlockSpec(memory_space=pl.ANY),
                  pl.BlockSpec((None, d, tn), lambda m, j, be, act: (be[m], 0, col(j, act, m))),
                  pl.BlockSpec((None, d, tn), lambda m, j, be, act: (be[m], 0, nj + col(j, act, m))),
                  pl.BlockSpec((None, 1, tn), lambda m, j, be, act: (be[m], 0, col(j, act, m))),
                  pl.BlockSpec((None, 1, tn), lambda m, j, be, act: (be[m], 0, nj + col(j, act, m)))],
        out_specs=pl.BlockSpec((bm, tn), lambda m, j, be, act: (m, j)),
        scratch_shapes=[pltpu.VMEM((bm, d), F32), pltpu.VMEM((bm, d), BF16), pltpu.SemaphoreType.DMA(())],
    )
    return pl.pallas_call(
        functools.partial(_gate_up_kernel, bm),
        out_shape=jax.ShapeDtypeStruct((nb * bm, f), BF16),
        grid_spec=grid_spec,
        compiler_params=_params(("arbitrary", "arbitrary")),
        name="moe_gate_up",
    )(block_e, active, slot_tok, h, w_gu_bf, w_gu_bf, b_gu, b_gu)


def _down_kernel(be_ref, act_ref, x_ref, w_ref, b_ref, o_ref):
    m = pl.program_id(0)
    active = act_ref[m] > 0

    @pl.when(active)
    def _compute():
        o_ref[...] = jnp.dot(x_ref[...], w_ref[...], preferred_element_type=F32) + b_ref[...]

    @pl.when(jnp.logical_not(active))
    def _idle():
        o_ref[...] = jnp.zeros_like(o_ref)


def _down(block_e, active, hmid, w_dn_bf, b_dn, bm, tn):
    nb = block_e.shape[0]
    f = hmid.shape[1]
    d = w_dn_bf.shape[2]
    nj = d // tn
    col = lambda j, act, m: jnp.where(act[m] > 0, j, nj - 1)
    grid_spec = pltpu.PrefetchScalarGridSpec(
        num_scalar_prefetch=2,
        grid=(nb, nj),
        in_specs=[pl.BlockSpec((bm, f), lambda m, j, be, act: (m, 0)),
                  pl.BlockSpec((None, f, tn), lambda m, j, be, act: (be[m], 0, col(j, act, m))),
                  pl.BlockSpec((None, 1, tn), lambda m, j, be, act: (be[m], 0, col(j, act, m)))],
        out_specs=pl.BlockSpec((bm, tn), lambda m, j, be, act: (m, j)),
    )
    return pl.pallas_call(
        _down_kernel,
        out_shape=jax.ShapeDtypeStruct((nb * bm, d), F32),
        grid_spec=grid_spec,
        compiler_params=_params(("arbitrary", "arbitrary")),
        name="moe_down",
    )(block_e, active, hmid, w_dn_bf, b_dn)


def _combine_kernel(alpha, tc, pos_ref, gate_ref, h_ref, ys_hbm, g_ref, b_ref, o_ref, buf_ref, sem):
    _gather_rows(pos_ref.at[0, 0], TOP_K * tc, ys_hbm, buf_ref, sem)
    gates = gate_ref[...]
    f = jnp.zeros(h_ref.shape, F32)
    for j in range(TOP_K):
        f = f + gates[:, j:j + 1] * buf_ref[j * tc:(j + 1) * tc, :]
    o_ref[...] = _layer_norm(alpha * h_ref[...] + f, g_ref[...], b_ref[...])


def _combine(pos_tiles, gates, h, ys, ln_g, ln_b, row0, n_rows, alpha, tc):
    d = h.shape[1]
    t0 = row0 // tc
    vec = pl.BlockSpec((1, d), lambda i: (0, 0))
    return pl.pallas_call(
        functools.partial(_combine_kernel, alpha, tc),
        out_shape=jax.ShapeDtypeStruct((n_rows, d), F32),
        grid=(n_rows // tc,),
        in_specs=[pl.BlockSpec((1, 1, TOP_K * tc), lambda i: (t0 + i, 0, 0), memory_space=pltpu.SMEM),
                  pl.BlockSpec((tc, LANES), lambda i: (t0 + i, 0)),
                  pl.BlockSpec((tc, d), lambda i: (t0 + i, 0)),
                  pl.BlockSpec(memory_space=pl.ANY), vec, vec],
        out_specs=pl.BlockSpec((tc, d), lambda i: (i, 0)),
        scratch_shapes=[pltpu.VMEM((TOP_K * tc, d), F32), pltpu.SemaphoreType.DMA(())],
        compiler_params=_params(("arbitrary",)),
        name="moe_combine_ln",
    )(pos_tiles, gates, h, ys, ln_g, ln_b)


def _rope_tables(seq_len):
    inv_freq = ROPE_THETA ** (-jnp.arange(0, ROT_DIM, 2, dtype=F32) / ROT_DIM)
    ang = jnp.arange(seq_len, dtype=F32)[:, None] * inv_freq[None, :]
    cos, sin = jnp.cos(ang), jnp.sin(ang)
    rest = HEAD_DIM - ROT_DIM
    cos_t = jnp.concatenate([cos, cos, jnp.ones((seq_len, rest), F32)], axis=1)
    sin_t = jnp.concatenate([-sin, sin, jnp.zeros((seq_len, rest), F32)], axis=1)
    return cos_t, sin_t


def _layer(xp, xs, shape_p, shape_s, lam_init, alpha, w_in, b_conv_in, conv_w, conv_b, conv_ln_g, conv_ln_b,
           lq1, lk1, lq2, lk2, subln_g, w_out, ln1_g, ln1_b, router_w, router_b, w_gu, b_gu, w_dn, b_dn, ln2_g, ln2_b):
    (bp, sp), (bs, ss) = shape_p, shape_s
    tp, d = xp.shape
    ts_ = xs.shape[0]
    t_all = tp + ts_
    d_attn = d // 2
    d_conv = d - d_attn
    n_heads = d_attn // HEAD_W
    n_exp = router_w.shape[1]
    row = lambda a: a.reshape(1, -1)

    tm = _tile(math.gcd(math.gcd(tp, ts_), math.gcd(sp, ss)), 512)
    tn = _tile(math.gcd(2 * d_attn, d_conv), 512)
    w_in_bf = w_in.astype(BF16)
    cos_t, sin_t = _rope_tables(max(sp, ss))
    qk = _inproj("qk", xp, xs, w_in_bf, 0, 2 * d_attn, tm, tn, (cos_t, sin_t, sp // tm, ss // tm), BF16)
    v = _inproj("v", xp, xs, w_in_bf, 2 * d_attn, d_attn, tm, tn, None, BF16)
    u = _inproj("glu", xp, xs, w_in_bf, 3 * d_attn, d_conv, tm, tn, row(b_conv_in), F32)

    lam_refs = (row(lq1), row(lk1), row(lq2), row(lk2))
    o_p = _attention(qk, v, lam_refs, row(subln_g), 0, bp, sp, n_heads, lam_init, _tile(sp, 256 if sp <= 4096 else 128))
    o_s = _attention(qk, v, lam_refs, row(subln_g), tp, bs, ss, n_heads, lam_init, _tile(ss, 256 if ss <= 4096 else 128))
    conv_args = (conv_w, row(conv_b), row(conv_ln_g), row(conv_ln_b))
    u_p = _conv(u, *conv_args, 0, bp, sp, _tile(math.gcd(sp, tp), 128))
    u_s = _conv(u, *conv_args, tp, bs, ss, _tile(math.gcd(ss, tp), 128))

    rw_pad = jnp.zeros((d, LANES), F32).at[:, :n_exp].set(router_w)
    rb_pad = jnp.zeros((1, LANES), F32).at[0, :n_exp].set(router_b)
    h, top_idx, gates, rank, counts = _outproj(
        o_p, o_s, u_p, u_s, w_out.astype(BF16), xp, xs, row(ln1_g), row(ln1_b), rw_pad, rb_pad, n_exp, alpha,
        tm, _tile(d_attn, 512))

    bm = _tile(t_all, 512)
    n_assign = t_all * TOP_K
    nb = n_assign // bm + n_exp
    cnt = counts[0, :n_exp]
    blocks = (cnt + bm - 1) // bm
    blocks_end = jnp.cumsum(blocks)
    first_slot = (blocks_end - blocks) * bm
    flat_e = top_idx[:, :TOP_K].reshape(-1)
    pos = jnp.take(first_slot, flat_e) + rank[:, :TOP_K].reshape(-1)
    slot_tok = jnp.zeros((nb * bm,), I32).at[pos].set(jnp.arange(n_assign, dtype=I32) // TOP_K, unique_indices=True)
    block_ids = jnp.arange(nb, dtype=I32)
    block_e = jnp.minimum(jnp.searchsorted(blocks_end, block_ids, side="right"), n_exp - 1).astype(I32)
    active = (block_ids < blocks_end[-1]).astype(I32)

    tf = _tile(w_dn.shape[1], 512)
    hmid = _gate_up(block_e, active, slot_tok.reshape(nb, 1, bm), h, w_gu.astype(BF16),
                    b_gu.reshape(n_exp, 1, -1), bm, tf)
    ys = _down(block_e, active, hmid, w_dn.astype(BF16), b_dn.reshape(n_exp, 1, -1), bm, _tile(d, 512))

    tc = _tile(math.gcd(tp, ts_), 256)
    pos_tiles = pos.reshape(t_all // tc, tc, TOP_K).transpose(0, 2, 1).reshape(t_all // tc, 1, TOP_K * tc)
    y_p = _combine(pos_tiles, gates, h, ys, row(ln2_g), row(ln2_b), 0, tp, alpha, tc)
    y_s = _combine(pos_tiles, gates, h, ys, row(ln2_g), row(ln2_b), tp, ts_, alpha, tc)
    return y_p, y_s


def kernel(x_prompt, x_sample, w_in, b_conv_in, conv_w, conv_b, conv_ln_g, conv_ln_b, lambda_q1, lambda_k1,
           lambda_q2, lambda_k2, subln_g, w_out, ln1_g, ln1_b, router_w, router_b, w_gate_up, b_gate_up, w_down,
           b_down, ln2_g, ln2_b):
    bp, sp, d = x_prompt.shape
    bs, ss, _ = x_sample.shape
    depth = w_in.shape[0]
    alpha = (2.0 * depth) ** 0.25
    xp = x_prompt.reshape(bp * sp, d)
    xs = x_sample.reshape(bs * ss, d)
    for l in range(depth):
        lam_init = 0.8 - 0.6 * math.exp(-0.3 * l)
        xp, xs = _layer(xp, xs, (bp, sp), (bs, ss), lam_init, alpha, w_in[l], b_conv_in[l], conv_w[l], conv_b[l],
                        conv_ln_g[l], conv_ln_b[l], lambda_q1[l], lambda_k1[l], lambda_q2[l], lambda_k2[l],
                        subln_g[l], w_out[l], ln1_g[l], ln1_b[l], router_w[l], router_b[l], w_gate_up[l],
                        b_gate_up[l], w_down[l], b_down[l], ln2_g[l], ln2_b[l])
    return xp.reshape(bp, sp, d), xs.reshape(bs, ss, d)
```

```python
import functools
import math

import jax
import jax.numpy as jnp
from jax import lax
from jax.experimental import pallas as pl
from jax.experimental.pallas import tpu as pltpu

F32 = jnp.float32
BF16 = jnp.bfloat16
I32 = jnp.int32

LANES = 128
SUBLANES = 8
HALO = 16
HEAD_DIM = 128
HEAD_W = 2 * HEAD_DIM
ROT_DIM = HEAD_DIM // 4
ROT_HALF = ROT_DIM // 2
ROPE_THETA = 500000.0
CONV_SIZE = 31
CONV_PAD = (CONV_SIZE - 1) // 2
TOP_K = 4
SWIGLU_LIMIT = 7.0
SWIGLU_ALPHA = 1.702
LN_EPS = 1e-5
LOG2E = 1.4426950408889634
VMEM_LIMIT = 56 * 1024 * 1024
ATTN_QUERY_TILE = 256
ATTN_KEY_CHUNK = 512


def _tile(dim, pref):
    t = pref
    while t > 1 and dim % t:
        t //= 2
    return t


def _params(sem, vmem=VMEM_LIMIT):
    return pltpu.CompilerParams(dimension_semantics=sem, vmem_limit_bytes=vmem)


def _layer_norm(z, g, b):
    mu = jnp.mean(z, axis=-1, keepdims=True)
    zc = z - mu
    var = jnp.mean(zc * zc, axis=-1, keepdims=True)
    return zc * lax.rsqrt(var + LN_EPS) * g + b


def _row_copy(i, n_p, tm, xp_hbm, xs_hbm, dst, sem):
    cp_p = pltpu.make_async_copy(xp_hbm.at[pl.ds(jnp.minimum(i, n_p - 1) * tm, tm), :], dst, sem)
    cp_s = pltpu.make_async_copy(xs_hbm.at[pl.ds(jnp.maximum(i - n_p, 0) * tm, tm), :], dst, sem)
    return cp_p, cp_s


def _row_start(i, n_p, tm, xp_hbm, xs_hbm, dst, sem):
    cp_p, cp_s = _row_copy(i, n_p, tm, xp_hbm, xs_hbm, dst, sem)
    pl.when(i < n_p)(cp_p.start)
    pl.when(i >= n_p)(cp_s.start)


def _row_wait(i, n_p, tm, xp_hbm, xs_hbm, dst, sem):
    cp_p, cp_s = _row_copy(i, n_p, tm, xp_hbm, xs_hbm, dst, sem)
    pl.when(i < n_p)(cp_p.wait)
    pl.when(i >= n_p)(cp_s.wait)


def _inproj_kernel(mode, n_p, tm, tn, xp_hbm, xs_hbm, *refs):
    if mode == "qk":
        w_ref, cos_ref, sin_ref, o_ref, xf_ref, xb_ref, sem = refs
    elif mode == "v":
        w_ref, o_ref, xf_ref, xb_ref, sem = refs
    else:
        wa_ref, wg_ref, ba_ref, bg_ref, o_ref, xf_ref, xb_ref, sem = refs
    i = pl.program_id(0)
    j = pl.program_id(1)

    @pl.when(j == 0)
    def _load_rows():
        _row_start(i, n_p, tm, xp_hbm, xs_hbm, xf_ref, sem)
        _row_wait(i, n_p, tm, xp_hbm, xs_hbm, xf_ref, sem)
        xb_ref[...] = xf_ref[...].astype(BF16)

    xb = xb_ref[...]
    if mode == "qk":
        acc = jnp.dot(xb, w_ref[...], preferred_element_type=F32)
        c = cos_ref[...]
        s = sin_ref[...]
        lane = lax.broadcasted_iota(I32, (tm, LANES), 1)
        for ch in range(tn // LANES):
            t = acc[:, ch * LANES:(ch + 1) * LANES]
            partner = jnp.where(lane < ROT_HALF, pltpu.roll(t, LANES - ROT_HALF, 1), pltpu.roll(t, ROT_HALF, 1))
            o_ref[:, ch * LANES:(ch + 1) * LANES] = (t * c + partner * s).astype(BF16)
    elif mode == "v":
        o_ref[...] = jnp.dot(xb, w_ref[...], preferred_element_type=F32).astype(BF16)
    else:
        a = jnp.dot(xb, wa_ref[...], preferred_element_type=F32) + ba_ref[...]
        g = jnp.dot(xb, wg_ref[...], preferred_element_type=F32) + bg_ref[...]
        o_ref[...] = a * (1.0 / (1.0 + jnp.exp(-g)))


def _inproj(mode, xp, xs, w_bf, col0, ncols, tm, tn, extra, out_dtype):
    tp, d = xp.shape
    t_all = tp + xs.shape[0]
    n_p = tp // tm
    cb0 = col0 // tn
    any_spec = pl.BlockSpec(memory_space=pl.ANY)
    w_spec = pl.BlockSpec((d, tn), lambda i, j: (0, cb0 + j))
    if mode == "qk":
        cos_t, sin_t, sp_t, ss_t = extra
        pos = lambda i, j: (jnp.where(i < n_p, i % sp_t, (i - n_p) % ss_t), 0)
        in_specs = [any_spec, any_spec, w_spec, pl.BlockSpec((tm, LANES), pos), pl.BlockSpec((tm, LANES), pos)]
        args = (xp, xs, w_bf, cos_t, sin_t)
    elif mode == "v":
        in_specs = [any_spec, any_spec, w_spec]
        args = (xp, xs, w_bf)
    else:
        bias = extra
        gb0 = (col0 + ncols) // tn
        in_specs = [any_spec, any_spec, w_spec, pl.BlockSpec((d, tn), lambda i, j: (0, gb0 + j)),
                    pl.BlockSpec((1, tn), lambda i, j: (0, j)),
                    pl.BlockSpec((1, tn), lambda i, j: (0, ncols // tn + j))]
        args = (xp, xs, w_bf, w_bf, bias, bias)
    return pl.pallas_call(
        functools.partial(_inproj_kernel, mode, n_p, tm, tn),
        out_shape=jax.ShapeDtypeStruct((t_all, ncols), out_dtype),
        grid=(t_all // tm, ncols // tn),
        in_specs=in_specs,
        out_specs=pl.BlockSpec((tm, tn), lambda i, j: (i, j)),
        scratch_shapes=[pltpu.VMEM((tm, d), F32), pltpu.VMEM((tm, d), BF16), pltpu.SemaphoreType.DMA(())],
        compiler_params=_params(("arbitrary", "arbitrary")),
        name="inproj_" + mode,
    )(*args)


def _lane_fold(x, op):
    parts = [x[:, t * LANES:(t + 1) * LANES] for t in range(x.shape[1] // LANES)]
    while len(parts) > 1:
        parts = [op(parts[t], parts[t + 1]) for t in range(0, len(parts) - 1, 2)] + parts[len(parts) & ~1:]
    return parts[0]


def _attn_kernel(lam_init, tk, q_ref, k_ref, v_ref, lq1_ref, lk1_ref, lq2_ref, lk2_ref, g_ref, o_ref,
                 s_ref, p_ref, fold_ref, acc_ref):
    tq = q_ref.shape[0]
    c = (HEAD_DIM ** -0.5) * LOG2E
    n_chunks = k_ref.shape[0] // tk
    half = n_chunks // 2

    def scores(mp):
        cols = slice(mp * HEAD_DIM, (mp + 1) * HEAD_DIM)
        q = q_ref[:, cols]
        fold = None
        for ch in range(n_chunks):
            keys = slice(ch * tk, (ch + 1) * tk)
            s = lax.dot_general(q, k_ref[keys, cols], (((1,), (1,)), ((), ())), preferred_element_type=F32)
            s_ref[mp, :, keys] = s
            part = _lane_fold(s, jnp.maximum)
            fold = part if fold is None else jnp.maximum(fold, part)
        fold_ref[mp] = fold

    def probs(mp, lo, hi):
        row_max = jnp.max(fold_ref[mp], axis=-1, keepdims=True)
        fold = None if lo == 0 else fold_ref[2 + mp]
        for ch in range(lo, hi):
            keys = slice(ch * tk, (ch + 1) * tk)
            p = jnp.exp2((s_ref[mp, :, keys] - row_max) * c)
            part = _lane_fold(p, jnp.add)
            fold = part if fold is None else fold + part
            p_ref[mp * tq:(mp + 1) * tq, keys] = p.astype(BF16)
        fold_ref[2 + mp] = fold

    def values(lo, hi):
        acc = None
        for ch in range(lo, hi):
            keys = slice(ch * tk, (ch + 1) * tk)
            pv = jnp.dot(p_ref[:, keys], v_ref[keys, :], preferred_element_type=F32)
            acc = pv if acc is None else acc + pv
        return acc

    scores(0)
    scores(1)
    probs(0, 0, n_chunks)
    probs(1, 0, half)
    acc_ref[...] = values(0, half)
    probs(1, half, n_chunks)
    acc = values(half, n_chunks) + acc_ref[...]
    l0 = jnp.sum(fold_ref[2], axis=-1, keepdims=True)
    l1 = jnp.sum(fold_ref[3], axis=-1, keepdims=True)
    lam = (jnp.exp(jnp.sum(lq1_ref[...] * lk1_ref[...], axis=-1, keepdims=True))
           - jnp.exp(jnp.sum(lq2_ref[...] * lk2_ref[...], axis=-1, keepdims=True)) + lam_init)
    o = acc[:tq] * (1.0 / l0) - acc[tq:] * (lam / l1)
    ms = jnp.mean(o * o, axis=-1, keepdims=True)
    o_ref[...] = (o * lax.rsqrt(ms + LN_EPS) * g_ref[...] * (1.0 - lam_init)).astype(BF16)


def _attention(qk, v, lam_refs, subln_g, row0, batch, seq, n_heads, lam_init, tq):
    assert row0 % seq == 0 and seq % tq == 0
    qt = seq // tq
    kcol0 = n_heads
    small = pl.BlockSpec((1, HEAD_DIM), lambda b, h, i: (0, 0))
    return pl.pallas_call(
        functools.partial(_attn_kernel, lam_init, _tile(seq // 2, ATTN_KEY_CHUNK)),
        out_shape=jax.ShapeDtypeStruct((batch * seq, n_heads * HEAD_W), BF16),
        grid=(batch, n_heads, qt),
        in_specs=[pl.BlockSpec((tq, HEAD_W), lambda b, h, i: (row0 // tq + b * qt + i, h)),
                  pl.BlockSpec((seq, HEAD_W), lambda b, h, i: (row0 // seq + b, kcol0 + h)),
                  pl.BlockSpec((seq, HEAD_W), lambda b, h, i: (row0 // seq + b, h)),
                  small, small, small, small,
                  pl.BlockSpec((1, HEAD_W), lambda b, h, i: (0, 0))],
        out_specs=pl.BlockSpec((tq, HEAD_W), lambda b, h, i: (b * qt + i, h)),
        scratch_shapes=[pltpu.VMEM((2, tq, seq), F32), pltpu.VMEM((2 * tq, seq), BF16),
                        pltpu.VMEM((4, tq, LANES), F32), pltpu.VMEM((2 * tq, HEAD_W), F32)],
        compiler_params=_params(("arbitrary", "arbitrary", "arbitrary")),
        name="diff_attention",
    )(qk, qk, v, *lam_refs, subln_g)


def _conv_kernel(ts, nt, rc, prev_ref, x_ref, next_ref, w_ref, cb_ref, g_ref, b_ref, o_ref, pad_ref, sh_ref, u_ref):
    t = pl.program_id(1)
    c_all = x_ref.shape[1]
    pad_ref[0:HALO, :] = jnp.where(t > 0, prev_ref[...], 0.0)
    pad_ref[HALO:HALO + ts, :] = x_ref[...]
    pad_ref[HALO + ts:, :] = jnp.where(t < nt - 1, next_ref[...], 0.0)
    span = sh_ref.shape[1]
    for s in range(1, SUBLANES):
        sh_ref[s - 1] = pad_ref[s:s + span, :]
    off = HALO - CONV_PAD
    for cc in range(c_all // LANES):
        cs = slice(cc * LANES, (cc + 1) * LANES)
        for r in range(ts // rc):
            acc = jnp.zeros((rc, LANES), F32)
            for k in range(CONV_SIZE):
                up, s = divmod(k + off, SUBLANES)
                lo = r * rc + up * SUBLANES
                rows = pad_ref[lo:lo + rc, cs] if s == 0 else sh_ref[s - 1, lo:lo + rc, cs]
                acc = acc + rows * w_ref[k:k + 1, cs]
            u_ref[r * rc:(r + 1) * rc, cs] = acc
    u = _layer_norm(u_ref[...] + cb_ref[...], g_ref[...], b_ref[...])
    o_ref[...] = (u * (1.0 / (1.0 + jnp.exp(-u)))).astype(BF16)


def _conv(u, conv_w, conv_b, ln_g, ln_b, row0, batch, seq, ts):
    c = u.shape[1]
    nt = seq // ts
    assert row0 % ts == 0 and seq % ts == 0 and ts % HALO == 0
    hb = ts // HALO
    last_h = u.shape[0] // HALO - 1
    base = lambda b, t: row0 // ts + b * nt + t
    vec = pl.BlockSpec((1, c), lambda b, t: (0, 0))
    return pl.pallas_call(
        functools.partial(_conv_kernel, ts, nt, _tile(ts, 32)),
        out_shape=jax.ShapeDtypeStruct((batch * seq, c), BF16),
        grid=(batch, nt),
        in_specs=[pl.BlockSpec((HALO, c), lambda b, t: (jnp.maximum(base(b, t) * hb - 1, 0), 0)),
                  pl.BlockSpec((ts, c), lambda b, t: (base(b, t), 0)),
                  pl.BlockSpec((HALO, c), lambda b, t: (jnp.minimum((base(b, t) + 1) * hb, last_h), 0)),
                  pl.BlockSpec((CONV_SIZE, c), lambda b, t: (0, 0)),
                  vec, vec, vec],
        out_specs=pl.BlockSpec((ts, c), lambda b, t: (b * nt + t, 0)),
        scratch_shapes=[pltpu.VMEM((ts + 2 * HALO, c), F32),
                        pltpu.VMEM((SUBLANES - 1, ts + 2 * HALO - SUBLANES, c), F32),
                        pltpu.VMEM((ts, c), F32)],
        compiler_params=_params(("arbitrary", "arbitrary")),
        name="conformer_conv",
    )(u, u, u, conv_w, conv_b, ln_g, ln_b)


def _outproj_kernel(alpha, n_exp, n_p, tm, nk, rc, *refs):
    (op_ref, os_ref, up_ref, us_ref, w_ref, xp_hbm, xs_hbm, g_ref, b_ref, rw_ref, rb_ref,
     h_ref, idx_ref, gate_ref, rank_ref, cnt_ref, acc_ref, x_ref, logit_ref, run_ref, sem) = refs
    i = pl.program_id(0)
    k = pl.program_id(1)
    half = nk // 2

    @pl.when(k == 0)
    def _init():
        acc_ref[...] = jnp.zeros_like(acc_ref)
        _row_start(i, n_p, tm, xp_hbm, xs_hbm, x_ref, sem)

    @pl.when((i == 0) & (k == 0))
    def _init_counts():
        run_ref[...] = jnp.zeros_like(run_ref)

    for src, (lo, is_p) in zip((op_ref, up_ref, os_ref, us_ref), ((0, True), (half, True), (0, False), (half, False))):
        rows = (i < n_p) if is_p else (i >= n_p)

        @pl.when(rows & (k >= lo) & (k < lo + half))
        def _acc(src=src):
            acc_ref[...] += jnp.dot(src[...], w_ref[...], preferred_element_type=F32)

    @pl.when(k == nk - 1)
    def _epilogue():
        _row_wait(i, n_p, tm, xp_hbm, xs_hbm, x_ref, sem)
        for r0 in range(0, tm, rc):
            h = _layer_norm(alpha * x_ref[r0:r0 + rc, :] + acc_ref[r0:r0 + rc, :], g_ref[...], b_ref[...])
            h_ref[r0:r0 + rc, :] = h
            h_hi = h.astype(BF16)
            h_lo = (h - h_hi.astype(F32)).astype(BF16)
            big = jnp.dot(h_hi, rw_ref[...], preferred_element_type=F32)
            small = jnp.dot(h_lo, rw_ref[...], preferred_element_type=F32)
            logit_ref[r0:r0 + rc, :] = big[:, :LANES] + (big[:, LANES:] + small[:, :LANES])
        lane = lax.broadcasted_iota(I32, (tm, LANES), 1)
        lane_f = lane.astype(F32)
        logits = jnp.where(lane < n_exp, logit_ref[...] + rb_ref[...], -jnp.inf)
        idx_out = jnp.zeros((tm, LANES), F32)
        val_out = jnp.zeros((tm, LANES), F32)
        member = jnp.zeros((tm, LANES), F32)
        picks = []
        top = None
        for j in range(TOP_K):
            mx = jnp.max(logits, axis=-1, keepdims=True)
            top = mx if top is None else top
            pick = jnp.min(jnp.where(logits == mx, lane_f, float(LANES)), axis=-1, keepdims=True)
            hit = lane_f == pick
            picks.append(hit)
            idx_out = jnp.where(lane == j, pick, idx_out)
            val_out = jnp.where(lane == j, mx, val_out)
            member = jnp.where(hit, 1.0, member)
            logits = jnp.where(hit, -jnp.inf, logits)
        e = jnp.where(lane < TOP_K, jnp.exp(val_out - top), 0.0)
        gate_ref[...] = e * (1.0 / jnp.sum(e, axis=-1, keepdims=True))
        idx_ref[...] = idx_out.astype(I32)
        row = lax.broadcasted_iota(I32, (tm, tm), 0)
        col = lax.broadcasted_iota(I32, (tm, tm), 1)
        before = jnp.where(col < row, 1.0, 0.0).astype(BF16)
        prior = jnp.dot(before, member.astype(BF16), preferred_element_type=F32) + run_ref[...]
        rank_out = jnp.zeros((tm, LANES), F32)
        for j in range(TOP_K):
            rj = jnp.sum(jnp.where(picks[j], prior, 0.0), axis=-1, keepdims=True)
            rank_out = jnp.where(lane == j, rj, rank_out)
        rank_ref[...] = rank_out.astype(I32)
        run_ref[...] += jnp.sum(member, axis=0, keepdims=True)
        cnt_ref[...] = run_ref[...].astype(I32)


def _outproj(o_p, o_s, u_p, u_s, w_out_bf, xp, xs, ln_g, ln_b, rw_pad, rb_pad, n_exp, alpha, tm, tk):
    tp, d = xp.shape
    t_all = tp + xs.shape[0]
    n_p = tp // tm
    n_s = xs.shape[0] // tm
    half = o_p.shape[1] // tk
    nk = 2 * half
    rowp = lambda i: jnp.minimum(i, n_p - 1)
    rows = lambda i: jnp.clip(i - n_p, 0, n_s - 1)
    attn_k = lambda k: jnp.minimum(k, half - 1)
    conv_k = lambda k: jnp.clip(k - half, 0, half - 1)
    any_spec = pl.BlockSpec(memory_space=pl.ANY)
    vec = pl.BlockSpec((1, d), lambda i, k: (0, 0))
    tok = pl.BlockSpec((tm, LANES), lambda i, k: (i, 0))
    return pl.pallas_call(
        functools.partial(_outproj_kernel, alpha, n_exp, n_p, tm, nk, _tile(tm, 128)),
        out_shape=(jax.ShapeDtypeStruct((t_all, d), F32),
                   jax.ShapeDtypeStruct((t_all, LANES), I32),
                   jax.ShapeDtypeStruct((t_all, LANES), F32),
                   jax.ShapeDtypeStruct((t_all, LANES), I32),
                   jax.ShapeDtypeStruct((1, LANES), I32)),
        grid=(t_all // tm, nk),
        in_specs=[pl.BlockSpec((tm, tk), lambda i, k: (rowp(i), attn_k(k))),
                  pl.BlockSpec((tm, tk), lambda i, k: (rows(i), attn_k(k))),
                  pl.BlockSpec((tm, tk), lambda i, k: (rowp(i), conv_k(k))),
                  pl.BlockSpec((tm, tk), lambda i, k: (rows(i), conv_k(k))),
                  pl.BlockSpec((tk, d), lambda i, k: (k, 0)),
                  any_spec, any_spec, vec, vec,
                  pl.BlockSpec((d, 2 * LANES), lambda i, k: (0, 0)),
                  pl.BlockSpec((1, LANES), lambda i, k: (0, 0))],
        out_specs=(pl.BlockSpec((tm, d), lambda i, k: (i, 0)), tok, tok, tok,
                   pl.BlockSpec((1, LANES), lambda i, k: (0, 0))),
        scratch_shapes=[pltpu.VMEM((tm, d), F32), pltpu.VMEM((tm, d), F32), pltpu.VMEM((tm, LANES), F32),
                        pltpu.VMEM((1, LANES), F32),
                        pltpu.SemaphoreType.DMA(())],
        compiler_params=_params(("arbitrary", "arbitrary")),
        name="outproj_ln_router",
    )(o_p, o_s, u_p, u_s, w_out_bf, xp, xs, ln_g, ln_b, rw_pad, rb_pad)


def _row_gather_start(tok_ref, r, src_hbm, dst_ref, sem):
    pltpu.make_async_copy(src_hbm.at[pl.ds(tok_ref[r], 1), :], dst_ref.at[pl.ds(r, 1), :], sem).start()


def _row_gather_wait(n, src_hbm, dst_ref, sem):
    pltpu.make_async_copy(src_hbm.at[pl.ds(0, n), :], dst_ref.at[pl.ds(0, n), :], sem).wait()


def _gather_rows(tok_ref, n, src_hbm, dst_ref, sem):
    def start(r, carry):
        _row_gather_start(tok_ref, r, src_hbm, dst_ref, sem)
        return carry

    lax.fori_loop(0, n, start, 0, unroll=8)
    _row_gather_wait(n, src_hbm, dst_ref, sem)


def _gate_up_kernel(bm, nj, be_ref, act_ref, tok_ref, nxt_ref, h_hbm, wg_ref, wu_ref, bg_ref, bu_ref, o_ref,
                    xf_ref, xb_ref, sem):
    m = pl.program_id(0)
    j = pl.program_id(1)
    active = act_ref[m] > 0
    slot = m % 2
    step_rows = bm // nj

    @pl.when((m == 0) & (j == 0))
    def _first_block():
        def start(r, carry):
            _row_gather_start(tok_ref.at[0, 0], r, h_hbm, xf_ref.at[0], sem.at[0])
            return carry
        lax.fori_loop(0, bm, start, 0, unroll=8)

    @pl.when(j == 0)
    def _rows_ready():
        prev_active = act_ref[jnp.maximum(m - 1, 0)] > 0

        @pl.when(active | prev_active)
        def _wait():
            _row_gather_wait(bm, h_hbm, xf_ref.at[slot], sem.at[slot])

        @pl.when(active)
        def _cast():
            xb_ref[...] = xf_ref[slot].astype(BF16)

    @pl.when(active)
    def _compute():
        for r in range(step_rows):
            _row_gather_start(nxt_ref.at[0, 0], j * step_rows + r, h_hbm, xf_ref.at[1 - slot], sem.at[1 - slot])
        xb = xb_ref[...]
        g = jnp.dot(xb, wg_ref[...], preferred_element_type=F32) + bg_ref[...]
        up = jnp.dot(xb, wu_ref[...], preferred_element_type=F32) + bu_ref[...]
        g = jnp.minimum(g, SWIGLU_LIMIT)
        up = jnp.clip(up, -SWIGLU_LIMIT, SWIGLU_LIMIT)
        o_ref[...] = ((up + 1.0) * (g * (1.0 / (1.0 + jnp.exp(-SWIGLU_ALPHA * g))))).astype(BF16)

    @pl.when(jnp.logical_not(active))
    def _idle():
        o_ref[...] = jnp.zeros_like(o_ref)


def _gate_up(block_e, active, slot_tok, h, w_gu_bf, b_gu, bm, tn):
    nb = block_e.shape[0]
    d = h.shape[1]
    f = w_gu_bf.shape[2] // 2
    nj = f // tn
    col = lambda j, act, m: jnp.where(act[m] > 0, j, nj - 1)
    grid_spec = pltpu.PrefetchScalarGridSpec(
        num_scalar_prefetch=2,
        grid=(nb, nj),
        in_specs=[pl.BlockSpec((1, 1, bm), lambda m, j, be, act: (m, 0, 0), memory_space=pltpu.SMEM),
                  pl.BlockSpec((1, 1, bm), lambda m, j, be, act: (jnp.minimum(m + 1, nb - 1), 0, 0),
                               memory_space=pltpu.SMEM),
                  pl.BlockSpec(memory_space=pl.ANY),
                  pl.BlockSpec((None, d, tn), lambda m, j, be, act: (be[m], 0, col(j, act, m))),
                  pl.BlockSpec((None, d, tn), lambda m, j, be, act: (be[m], 0, nj + col(j, act, m))),
                  pl.BlockSpec((None, 1, tn), lambda m, j, be, act: (be[m], 0, col(j, act, m))),
                  pl.BlockSpec((None, 1, tn), lambda m, j, be, act: (be[m], 0, nj + col(j, act, m)))],
        out_specs=pl.BlockSpec((bm, tn), lambda m, j, be, act: (m, j)),
        scratch_shapes=[pltpu.VMEM((2, bm, d), F32), pltpu.VMEM((bm, d), BF16), pltpu.SemaphoreType.DMA((2,))],
    )
    return pl.pallas_call(
        functools.partial(_gate_up_kernel, bm, nj),
        out_shape=jax.ShapeDtypeStruct((nb * bm, f), BF16),
        grid_spec=grid_spec,
        compiler_params=_params(("arbitrary", "arbitrary")),
        name="moe_gate_up",
    )(block_e, active, slot_tok, slot_tok, h, w_gu_bf, w_gu_bf, b_gu, b_gu)


def _down_kernel(be_ref, act_ref, x_ref, w_ref, b_ref, o_ref):
    m = pl.program_id(0)
    active = act_ref[m] > 0

    @pl.when(active)
    def _compute():
        o_ref[...] = jnp.dot(x_ref[...], w_ref[...], preferred_element_type=F32) + b_ref[...]

    @pl.when(jnp.logical_not(active))
    def _idle():
        o_ref[...] = jnp.zeros_like(o_ref)


def _down(block_e, active, hmid, w_dn_bf, b_dn, bm, tn):
    nb = block_e.shape[0]
    f = hmid.shape[1]
    d = w_dn_bf.shape[2]
    nj = d // tn
    col = lambda j, act, m: jnp.where(act[m] > 0, j, nj - 1)
    grid_spec = pltpu.PrefetchScalarGridSpec(
        num_scalar_prefetch=2,
        grid=(nb, nj),
        in_specs=[pl.BlockSpec((bm, f), lambda m, j, be, act: (m, 0)),
                  pl.BlockSpec((None, f, tn), lambda m, j, be, act: (be[m], 0, col(j, act, m))),
                  pl.BlockSpec((None, 1, tn), lambda m, j, be, act: (be[m], 0, col(j, act, m)))],
        out_specs=pl.BlockSpec((bm, tn), lambda m, j, be, act: (m, j)),
    )
    return pl.pallas_call(
        _down_kernel,
        out_shape=jax.ShapeDtypeStruct((nb * bm, d), F32),
        grid_spec=grid_spec,
        compiler_params=_params(("arbitrary", "arbitrary")),
        name="moe_down",
    )(block_e, active, hmid, w_dn_bf, b_dn)


def _combine_kernel(alpha, tc, pos_ref, gate_ref, h_ref, ys_hbm, g_ref, b_ref, o_ref, buf_ref, sem):
    _gather_rows(pos_ref.at[0, 0], TOP_K * tc, ys_hbm, buf_ref, sem)
    gates = gate_ref[...]
    f = jnp.zeros(h_ref.shape, F32)
    for j in range(TOP_K):
        f = f + gates[:, j:j + 1] * buf_ref[j * tc:(j + 1) * tc, :]
    o_ref[...] = _layer_norm(alpha * h_ref[...] + f, g_ref[...], b_ref[...])


def _combine(pos_tiles, gates, h, ys, ln_g, ln_b, row0, n_rows, alpha, tc):
    d = h.shape[1]
    t0 = row0 // tc
    vec = pl.BlockSpec((1, d), lambda i: (0, 0))
    return pl.pallas_call(
        functools.partial(_combine_kernel, alpha, tc),
        out_shape=jax.ShapeDtypeStruct((n_rows, d), F32),
        grid=(n_rows // tc,),
        in_specs=[pl.BlockSpec((1, 1, TOP_K * tc), lambda i: (t0 + i, 0, 0), memory_space=pltpu.SMEM),
                  pl.BlockSpec((tc, LANES), lambda i: (t0 + i, 0)),
                  pl.BlockSpec((tc, d), lambda i: (t0 + i, 0)),
                  pl.BlockSpec(memory_space=pl.ANY), vec, vec],
        out_specs=pl.BlockSpec((tc, d), lambda i: (i, 0)),
        scratch_shapes=[pltpu.VMEM((TOP_K * tc, d), F32), pltpu.SemaphoreType.DMA(())],
        compiler_params=_params(("arbitrary",)),
        name="moe_combine_ln",
    )(pos_tiles, gates, h, ys, ln_g, ln_b)


def _rope_tables(seq_len):
    inv_freq = ROPE_THETA ** (-jnp.arange(0, ROT_DIM, 2, dtype=F32) / ROT_DIM)
    ang = jnp.arange(seq_len, dtype=F32)[:, None] * inv_freq[None, :]
    cos, sin = jnp.cos(ang), jnp.sin(ang)
    rest = HEAD_DIM - ROT_DIM
    cos_t = jnp.concatenate([cos, cos, jnp.ones((seq_len, rest), F32)], axis=1)
    sin_t = jnp.concatenate([-sin, sin, jnp.zeros((seq_len, rest), F32)], axis=1)
    return cos_t, sin_t


def _layer(xp, xs, shape_p, shape_s, lam_init, alpha, w_in, b_conv_in, conv_w, conv_b, conv_ln_g, conv_ln_b,
           lq1, lk1, lq2, lk2, subln_g, w_out, ln1_g, ln1_b, router_w, router_b, w_gu, b_gu, w_dn, b_dn, ln2_g, ln2_b):
    (bp, sp), (bs, ss) = shape_p, shape_s
    tp, d = xp.shape
    ts_ = xs.shape[0]
    t_all = tp + ts_
    d_attn = d // 2
    d_conv = d - d_attn
    n_heads = d_attn // HEAD_W
    n_exp = router_w.shape[1]
    row = lambda a: a.reshape(1, -1)

    tm = _tile(math.gcd(math.gcd(tp, ts_), math.gcd(sp, ss)), 512)
    tn = _tile(math.gcd(2 * d_attn, d_conv), 512)
    w_in_bf = w_in.astype(BF16)
    cos_t, sin_t = _rope_tables(max(sp, ss))
    qk = _inproj("qk", xp, xs, w_in_bf, 0, 2 * d_attn, tm, tn, (cos_t, sin_t, sp // tm, ss // tm), BF16)
    v = _inproj("v", xp, xs, w_in_bf, 2 * d_attn, d_attn, tm, tn, None, BF16)
    u = _inproj("glu", xp, xs, w_in_bf, 3 * d_attn, d_conv, tm, tn, row(b_conv_in), F32)

    lam_refs = (row(lq1), row(lk1), row(lq2), row(lk2))
    o_p = _attention(qk, v, lam_refs, row(subln_g), 0, bp, sp, n_heads, lam_init, _tile(sp, ATTN_QUERY_TILE))
    o_s = _attention(qk, v, lam_refs, row(subln_g), tp, bs, ss, n_heads, lam_init, _tile(ss, ATTN_QUERY_TILE))
    conv_args = (conv_w, row(conv_b), row(conv_ln_g), row(conv_ln_b))
    u_p = _conv(u, *conv_args, 0, bp, sp, _tile(math.gcd(sp, tp), 128))
    u_s = _conv(u, *conv_args, tp, bs, ss, _tile(math.gcd(ss, tp), 128))

    rw_f32 = jnp.zeros((d, LANES), F32).at[:, :n_exp].set(router_w)
    rw_hi = rw_f32.astype(BF16)
    rw_pad = jnp.concatenate([rw_hi, (rw_f32 - rw_hi.astype(F32)).astype(BF16)], axis=1)
    rb_pad = jnp.zeros((1, LANES), F32).at[0, :n_exp].set(router_b)
    h, top_idx, gates, rank, counts = _outproj(
        o_p, o_s, u_p, u_s, w_out.astype(BF16), xp, xs, row(ln1_g), row(ln1_b), rw_pad, rb_pad, n_exp, alpha,
        tm, _tile(d_attn, 512))

    bm = _tile(t_all, 512)
    n_assign = t_all * TOP_K
    nb = n_assign // bm + n_exp
    cnt = counts[0, :n_exp]
    blocks = (cnt + bm - 1) // bm
    blocks_end = jnp.cumsum(blocks)
    first_slot = (blocks_end - blocks) * bm
    flat_e = top_idx[:, :TOP_K].reshape(-1)
    pos = jnp.take(first_slot, flat_e) + rank[:, :TOP_K].reshape(-1)
    slot_tok = jnp.zeros((nb * bm,), I32).at[pos].set(jnp.arange(n_assign, dtype=I32) // TOP_K, unique_indices=True)
    block_ids = jnp.arange(nb, dtype=I32)
    block_e = jnp.minimum(jnp.searchsorted(blocks_end, block_ids, side="right"), n_exp - 1).astype(I32)
    active = (block_ids < blocks_end[-1]).astype(I32)

    tf = _tile(w_dn.shape[1], 512)
    hmid = _gate_up(block_e, active, slot_tok.reshape(nb, 1, bm), h, w_gu.astype(BF16),
                    b_gu.reshape(n_exp, 1, -1), bm, tf)
    ys = _down(block_e, active, hmid, w_dn.astype(BF16), b_dn.reshape(n_exp, 1, -1), bm, _tile(d, 1024))

    tc = _tile(math.gcd(tp, ts_), 256)
    pos_tiles = pos.reshape(t_all // tc, tc, TOP_K).transpose(0, 2, 1).reshape(t_all // tc, 1, TOP_K * tc)
    y_p = _combine(pos_tiles, gates, h, ys, row(ln2_g), row(ln2_b), 0, tp, alpha, tc)
    y_s = _combine(pos_tiles, gates, h, ys, row(ln2_g), row(ln2_b), tp, ts_, alpha, tc)
    return y_p, y_s


def kernel(x_prompt, x_sample, w_in, b_conv_in, conv_w, conv_b, conv_ln_g, conv_ln_b, lambda_q1, lambda_k1,
           lambda_q2, lambda_k2, subln_g, w_out, ln1_g, ln1_b, router_w, router_b, w_gate_up, b_gate_up, w_down,
           b_down, ln2_g, ln2_b):
    bp, sp, d = x_prompt.shape
    bs, ss, _ = x_sample.shape
    depth = w_in.shape[0]
    alpha = (2.0 * depth) ** 0.25
    xp = x_prompt.reshape(bp * sp, d)
    xs = x_sample.reshape(bs * ss, d)
    for l in range(depth):
        lam_init = 0.8 - 0.6 * math.exp(-0.3 * l)
        xp, xs = _layer(xp, xs, (bp, sp), (bs, ss), lam_init, alpha, w_in[l], b_conv_in[l], conv_w[l], conv_b[l],
                        conv_ln_g[l], conv_ln_b[l], lambda_q1[l], lambda_k1[l], lambda_q2[l], lambda_k2[l],
                        subln_g[l], w_out[l], ln1_g[l], ln1_b[l], router_w[l], router_b[l], w_gate_up[l],
                        b_gate_up[l], w_down[l], b_down[l], ln2_g[l], ln2_b[l])
    return xp.reshape(bp, sp, d), xs.reshape(bs, ss, d)
```

```python
import functools
import math

import jax
import jax.numpy as jnp
from jax import lax
from jax.experimental import pallas as pl
from jax.experimental.pallas import tpu as pltpu

F32 = jnp.float32
BF16 = jnp.bfloat16
I32 = jnp.int32

LANES = 128
SUBLANES = 8
HALO = 16
HEAD_DIM = 128
HEAD_W = 2 * HEAD_DIM
ROT_DIM = HEAD_DIM // 4
ROT_HALF = ROT_DIM // 2
ROPE_THETA = 500000.0
CONV_SIZE = 31
CONV_PAD = (CONV_SIZE - 1) // 2
TOP_K = 4
SWIGLU_LIMIT = 7.0
SWIGLU_ALPHA = 1.702
LN_EPS = 1e-5
LOG2E = 1.4426950408889634
VMEM_LIMIT = 56 * 1024 * 1024
ATTN_QUERY_TILE = 256
ATTN_KEY_CHUNK = 512


def _tile(dim, pref):
    t = pref
    while t > 1 and dim % t:
        t //= 2
    return t


def _params(sem, vmem=VMEM_LIMIT):
    return pltpu.CompilerParams(dimension_semantics=sem, vmem_limit_bytes=vmem)


def _layer_norm(z, g, b):
    mu = jnp.mean(z, axis=-1, keepdims=True)
    zc = z - mu
    var = jnp.mean(zc * zc, axis=-1, keepdims=True)
    return zc * lax.rsqrt(var + LN_EPS) * g + b


def _row_copy(i, n_p, tm, xp_hbm, xs_hbm, dst, sem):
    cp_p = pltpu.make_async_copy(xp_hbm.at[pl.ds(jnp.minimum(i, n_p - 1) * tm, tm), :], dst, sem)
    cp_s = pltpu.make_async_copy(xs_hbm.at[pl.ds(jnp.maximum(i - n_p, 0) * tm, tm), :], dst, sem)
    return cp_p, cp_s


def _row_start(i, n_p, tm, xp_hbm, xs_hbm, dst, sem):
    cp_p, cp_s = _row_copy(i, n_p, tm, xp_hbm, xs_hbm, dst, sem)
    pl.when(i < n_p)(cp_p.start)
    pl.when(i >= n_p)(cp_s.start)


def _row_wait(i, n_p, tm, xp_hbm, xs_hbm, dst, sem):
    cp_p, cp_s = _row_copy(i, n_p, tm, xp_hbm, xs_hbm, dst, sem)
    pl.when(i < n_p)(cp_p.wait)
    pl.when(i >= n_p)(cp_s.wait)


def _inproj_kernel(mode, n_p, tm, tn, xp_hbm, xs_hbm, *refs):
    if mode == "qk":
        w_ref, cos_ref, sin_ref, o_ref, xf_ref, xb_ref, sem = refs
    elif mode == "v":
        w_ref, o_ref, xf_ref, xb_ref, sem = refs
    else:
        wa_ref, wg_ref, ba_ref, bg_ref, o_ref, xf_ref, xb_ref, sem = refs
    i = pl.program_id(0)
    j = pl.program_id(1)

    @pl.when(j == 0)
    def _load_rows():
        slot = i % 2

        @pl.when(i == 0)
        def _first():
            _row_start(i, n_p, tm, xp_hbm, xs_hbm, xf_ref.at[0], sem.at[0])

        _row_wait(i, n_p, tm, xp_hbm, xs_hbm, xf_ref.at[slot], sem.at[slot])

        @pl.when(i + 1 < pl.num_programs(0))
        def _next():
            _row_start(i + 1, n_p, tm, xp_hbm, xs_hbm, xf_ref.at[1 - slot], sem.at[1 - slot])

        xb_ref[...] = xf_ref[slot].astype(BF16)

    xb = xb_ref[...]
    if mode == "qk":
        acc = jnp.dot(xb, w_ref[...], preferred_element_type=F32)
        c = cos_ref[...]
        s = sin_ref[...]
        lane = lax.broadcasted_iota(I32, (tm, LANES), 1)
        for ch in range(tn // LANES):
            t = acc[:, ch * LANES:(ch + 1) * LANES]
            partner = jnp.where(lane < ROT_HALF, pltpu.roll(t, LANES - ROT_HALF, 1), pltpu.roll(t, ROT_HALF, 1))
            o_ref[:, ch * LANES:(ch + 1) * LANES] = (t * c + partner * s).astype(BF16)
    elif mode == "v":
        o_ref[...] = jnp.dot(xb, w_ref[...], preferred_element_type=F32).astype(BF16)
    else:
        a = jnp.dot(xb, wa_ref[...], preferred_element_type=F32) + ba_ref[...]
        g = jnp.dot(xb, wg_ref[...], preferred_element_type=F32) + bg_ref[...]
        o_ref[...] = a * (1.0 / (1.0 + jnp.exp(-g)))


def _inproj(mode, xp, xs, w_bf, col0, ncols, tm, tn, extra, out_dtype):
    tp, d = xp.shape
    t_all = tp + xs.shape[0]
    n_p = tp // tm
    cb0 = col0 // tn
    any_spec = pl.BlockSpec(memory_space=pl.ANY)
    w_spec = pl.BlockSpec((d, tn), lambda i, j: (0, cb0 + j))
    if mode == "qk":
        cos_t, sin_t, sp_t, ss_t = extra
        pos = lambda i, j: (jnp.where(i < n_p, i % sp_t, (i - n_p) % ss_t), 0)
        in_specs = [any_spec, any_spec, w_spec, pl.BlockSpec((tm, LANES), pos), pl.BlockSpec((tm, LANES), pos)]
        args = (xp, xs, w_bf, cos_t, sin_t)
    elif mode == "v":
        in_specs = [any_spec, any_spec, w_spec]
        args = (xp, xs, w_bf)
    else:
        bias = extra
        gb0 = (col0 + ncols) // tn
        in_specs = [any_spec, any_spec, w_spec, pl.BlockSpec((d, tn), lambda i, j: (0, gb0 + j)),
                    pl.BlockSpec((1, tn), lambda i, j: (0, j)),
                    pl.BlockSpec((1, tn), lambda i, j: (0, ncols // tn + j))]
        args = (xp, xs, w_bf, w_bf, bias, bias)
    return pl.pallas_call(
        functools.partial(_inproj_kernel, mode, n_p, tm, tn),
        out_shape=jax.ShapeDtypeStruct((t_all, ncols), out_dtype),
        grid=(t_all // tm, ncols // tn),
        in_specs=in_specs,
        out_specs=pl.BlockSpec((tm, tn), lambda i, j: (i, j)),
        scratch_shapes=[pltpu.VMEM((2, tm, d), F32), pltpu.VMEM((tm, d), BF16), pltpu.SemaphoreType.DMA((2,))],
        compiler_params=_params(("arbitrary", "arbitrary")),
        name="inproj_" + mode,
    )(*args)


def _lane_fold(x, op):
    parts = [x[:, t * LANES:(t + 1) * LANES] for t in range(x.shape[1] // LANES)]
    while len(parts) > 1:
        parts = [op(parts[t], parts[t + 1]) for t in range(0, len(parts) - 1, 2)] + parts[len(parts) & ~1:]
    return parts[0]


def _attn_kernel(lam_init, tk, q_ref, k_ref, v_ref, lq1_ref, lk1_ref, lq2_ref, lk2_ref, g_ref, o_ref,
                 s_ref, p_ref, fold_ref, acc_ref):
    tq = q_ref.shape[0]
    c = (HEAD_DIM ** -0.5) * LOG2E
    n_chunks = k_ref.shape[0] // tk
    half = n_chunks // 2

    def scores(mp):
        cols = slice(mp * HEAD_DIM, (mp + 1) * HEAD_DIM)
        q = q_ref[:, cols]
        fold = None
        for ch in range(n_chunks):
            keys = slice(ch * tk, (ch + 1) * tk)
            s = lax.dot_general(q, k_ref[keys, cols], (((1,), (1,)), ((), ())), preferred_element_type=F32)
            s_ref[mp, :, keys] = s
            part = _lane_fold(s, jnp.maximum)
            fold = part if fold is None else jnp.maximum(fold, part)
        fold_ref[mp] = fold

    def probs(mp, lo, hi):
        row_max = jnp.max(fold_ref[mp], axis=-1, keepdims=True)
        fold = None if lo == 0 else fold_ref[2 + mp]
        for ch in range(lo, hi):
            keys = slice(ch * tk, (ch + 1) * tk)
            p = jnp.exp2((s_ref[mp, :, keys] - row_max) * c)
            part = _lane_fold(p, jnp.add)
            fold = part if fold is None else fold + part
            p_ref[mp * tq:(mp + 1) * tq, keys] = p.astype(BF16)
        fold_ref[2 + mp] = fold

    def values(lo, hi):
        acc = None
        for ch in range(lo, hi):
            keys = slice(ch * tk, (ch + 1) * tk)
            pv = jnp.dot(p_ref[:, keys], v_ref[keys, :], preferred_element_type=F32)
            acc = pv if acc is None else acc + pv
        return acc

    scores(0)
    scores(1)
    probs(0, 0, n_chunks)
    probs(1, 0, half)
    acc_ref[...] = values(0, half)
    probs(1, half, n_chunks)
    acc = values(half, n_chunks) + acc_ref[...]
    l0 = jnp.sum(fold_ref[2], axis=-1, keepdims=True)
    l1 = jnp.sum(fold_ref[3], axis=-1, keepdims=True)
    lam = (jnp.exp(jnp.sum(lq1_ref[...] * lk1_ref[...], axis=-1, keepdims=True))
           - jnp.exp(jnp.sum(lq2_ref[...] * lk2_ref[...], axis=-1, keepdims=True)) + lam_init)
    o = acc[:tq] * (1.0 / l0) - acc[tq:] * (lam / l1)
    ms = jnp.mean(o * o, axis=-1, keepdims=True)
    o_ref[...] = (o * lax.rsqrt(ms + LN_EPS) * g_ref[...] * (1.0 - lam_init)).astype(BF16)


def _attention(qk, v, lam_refs, subln_g, row0, batch, seq, n_heads, lam_init, tq):
    assert row0 % seq == 0 and seq % tq == 0
    qt = seq // tq
    kcol0 = n_heads
    small = pl.BlockSpec((1, HEAD_DIM), lambda b, h, i: (0, 0))
    return pl.pallas_call(
        functools.partial(_attn_kernel, lam_init, _tile(seq // 2, ATTN_KEY_CHUNK)),
        out_shape=jax.ShapeDtypeStruct((batch * seq, n_heads * HEAD_W), BF16),
        grid=(batch, n_heads, qt),
        in_specs=[pl.BlockSpec((tq, HEAD_W), lambda b, h, i: (row0 // tq + b * qt + i, h)),
                  pl.BlockSpec((seq, HEAD_W), lambda b, h, i: (row0 // seq + b, kcol0 + h)),
                  pl.BlockSpec((seq, HEAD_W), lambda b, h, i: (row0 // seq + b, h)),
                  small, small, small, small,
                  pl.BlockSpec((1, HEAD_W), lambda b, h, i: (0, 0))],
        out_specs=pl.BlockSpec((tq, HEAD_W), lambda b, h, i: (b * qt + i, h)),
        scratch_shapes=[pltpu.VMEM((2, tq, seq), F32), pltpu.VMEM((2 * tq, seq), BF16),
                        pltpu.VMEM((4, tq, LANES), F32), pltpu.VMEM((2 * tq, HEAD_W), F32)],
        compiler_params=_params(("arbitrary", "arbitrary", "arbitrary")),
        name="diff_attention",
    )(qk, qk, v, *lam_refs, subln_g)


def _conv_kernel(ts, nt, rc, prev_ref, x_ref, next_ref, w_ref, cb_ref, g_ref, b_ref, o_ref, pad_ref, sh_ref, u_ref):
    t = pl.program_id(1)
    c_all = x_ref.shape[1]
    pad_ref[0:HALO, :] = jnp.where(t > 0, prev_ref[...], 0.0)
    pad_ref[HALO:HALO + ts, :] = x_ref[...]
    pad_ref[HALO + ts:, :] = jnp.where(t < nt - 1, next_ref[...], 0.0)
    span = sh_ref.shape[1]
    for s in range(1, SUBLANES):
        sh_ref[s - 1] = pad_ref[s:s + span, :]
    off = HALO - CONV_PAD
    for cc in range(c_all // LANES):
        cs = slice(cc * LANES, (cc + 1) * LANES)
        for r in range(ts // rc):
            acc = jnp.zeros((rc, LANES), F32)
            for k in range(CONV_SIZE):
                up, s = divmod(k + off, SUBLANES)
                lo = r * rc + up * SUBLANES
                rows = pad_ref[lo:lo + rc, cs] if s == 0 else sh_ref[s - 1, lo:lo + rc, cs]
                acc = acc + rows * w_ref[k:k + 1, cs]
            u_ref[r * rc:(r + 1) * rc, cs] = acc
    u = _layer_norm(u_ref[...] + cb_ref[...], g_ref[...], b_ref[...])
    o_ref[...] = (u * (1.0 / (1.0 + jnp.exp(-u)))).astype(BF16)


def _conv(u, conv_w, conv_b, ln_g, ln_b, row0, batch, seq, ts):
    c = u.shape[1]
    nt = seq // ts
    assert row0 % ts == 0 and seq % ts == 0 and ts % HALO == 0
    hb = ts // HALO
    last_h = u.shape[0] // HALO - 1
    base = lambda b, t: row0 // ts + b * nt + t
    vec = pl.BlockSpec((1, c), lambda b, t: (0, 0))
    return pl.pallas_call(
        functools.partial(_conv_kernel, ts, nt, _tile(ts, 32)),
        out_shape=jax.ShapeDtypeStruct((batch * seq, c), BF16),
        grid=(batch, nt),
        in_specs=[pl.BlockSpec((HALO, c), lambda b, t: (jnp.maximum(base(b, t) * hb - 1, 0), 0)),
                  pl.BlockSpec((ts, c), lambda b, t: (base(b, t), 0)),
                  pl.BlockSpec((HALO, c), lambda b, t: (jnp.minimum((base(b, t) + 1) * hb, last_h), 0)),
                  pl.BlockSpec((CONV_SIZE, c), lambda b, t: (0, 0)),
                  vec, vec, vec],
        out_specs=pl.BlockSpec((ts, c), lambda b, t: (b * nt + t, 0)),
        scratch_shapes=[pltpu.VMEM((ts + 2 * HALO, c), F32),
                        pltpu.VMEM((SUBLANES - 1, ts + 2 * HALO - SUBLANES, c), F32),
                        pltpu.VMEM((ts, c), F32)],
        compiler_params=_params(("arbitrary", "arbitrary")),
        name="conformer_conv",
    )(u, u, u, conv_w, conv_b, ln_g, ln_b)


def _outproj_kernel(alpha, n_exp, n_p, tm, nk, rc, *refs):
    (op_ref, os_ref, up_ref, us_ref, w_ref, xp_hbm, xs_hbm, g_ref, b_ref, rw_ref, rb_ref,
     h_ref, idx_ref, gate_ref, rank_ref, cnt_ref, acc_ref, x_ref, logit_ref, run_ref, sem) = refs
    i = pl.program_id(0)
    k = pl.program_id(1)
    half = nk // 2

    @pl.when(k == 0)
    def _init():
        acc_ref[...] = jnp.zeros_like(acc_ref)
        _row_start(i, n_p, tm, xp_hbm, xs_hbm, x_ref, sem)

    @pl.when((i == 0) & (k == 0))
    def _init_counts():
        run_ref[...] = jnp.zeros_like(run_ref)

    for src, (lo, is_p) in zip((op_ref, up_ref, os_ref, us_ref), ((0, True), (half, True), (0, False), (half, False))):
        rows = (i < n_p) if is_p else (i >= n_p)

        @pl.when(rows & (k >= lo) & (k < lo + half))
        def _acc(src=src):
            acc_ref[...] += jnp.dot(src[...], w_ref[...], preferred_element_type=F32)

    @pl.when(k == nk - 1)
    def _epilogue():
        _row_wait(i, n_p, tm, xp_hbm, xs_hbm, x_ref, sem)
        for r0 in range(0, tm, rc):
            h = _layer_norm(alpha * x_ref[r0:r0 + rc, :] + acc_ref[r0:r0 + rc, :], g_ref[...], b_ref[...])
            h_ref[r0:r0 + rc, :] = h
            h_hi = h.astype(BF16)
            h_lo = (h - h_hi.astype(F32)).astype(BF16)
            big = jnp.dot(h_hi, rw_ref[...], preferred_element_type=F32)
            small = jnp.dot(h_lo, rw_ref[...], preferred_element_type=F32)
            logit_ref[r0:r0 + rc, :] = big[:, :LANES] + (big[:, LANES:] + small[:, :LANES])
        lane = lax.broadcasted_iota(I32, (tm, LANES), 1)
        lane_f = lane.astype(F32)
        logits = jnp.where(lane < n_exp, logit_ref[...] + rb_ref[...], -jnp.inf)
        idx_out = jnp.zeros((tm, LANES), F32)
        val_out = jnp.zeros((tm, LANES), F32)
        member = jnp.zeros((tm, LANES), F32)
        picks = []
        top = None
        for j in range(TOP_K):
            mx = jnp.max(logits, axis=-1, keepdims=True)
            top = mx if top is None else top
            pick = jnp.min(jnp.where(logits == mx, lane_f, float(LANES)), axis=-1, keepdims=True)
            hit = lane_f == pick
            picks.append(hit)
            idx_out = jnp.where(lane == j, pick, idx_out)
            val_out = jnp.where(lane == j, mx, val_out)
            member = jnp.where(hit, 1.0, member)
            logits = jnp.where(hit, -jnp.inf, logits)
        e = jnp.where(lane < TOP_K, jnp.exp(val_out - top), 0.0)
        gate_ref[...] = e * (1.0 / jnp.sum(e, axis=-1, keepdims=True))
        idx_ref[...] = idx_out.astype(I32)
        row = lax.broadcasted_iota(I32, (tm, tm), 0)
        col = lax.broadcasted_iota(I32, (tm, tm), 1)
        before = jnp.where(col < row, 1.0, 0.0).astype(BF16)
        prior = jnp.dot(before, member.astype(BF16), preferred_element_type=F32) + run_ref[...]
        rank_out = jnp.zeros((tm, LANES), F32)
        for j in range(TOP_K):
            rj = jnp.sum(jnp.where(picks[j], prior, 0.0), axis=-1, keepdims=True)
            rank_out = jnp.where(lane == j, rj, rank_out)
        rank_ref[...] = rank_out.astype(I32)
        run_ref[...] += jnp.sum(member, axis=0, keepdims=True)
        cnt_ref[...] = run_ref[...].astype(I32)


def _outproj(o_p, o_s, u_p, u_s, w_out_bf, xp, xs, ln_g, ln_b, rw_pad, rb_pad, n_exp, alpha, tm, tk):
    tp, d = xp.shape
    t_all = tp + xs.shape[0]
    n_p = tp // tm
    n_s = xs.shape[0] // tm
    half = o_p.shape[1] // tk
    nk = 2 * half
    rowp = lambda i: jnp.minimum(i, n_p - 1)
    rows = lambda i: jnp.clip(i - n_p, 0, n_s - 1)
    attn_k = lambda k: jnp.minimum(k, half - 1)
    conv_k = lambda k: jnp.clip(k - half, 0, half - 1)
    any_spec = pl.BlockSpec(memory_space=pl.ANY)
    vec = pl.BlockSpec((1, d), lambda i, k: (0, 0))
    tok = pl.BlockSpec((tm, LANES), lambda i, k: (i, 0))
    return pl.pallas_call(
        functools.partial(_outproj_kernel, alpha, n_exp, n_p, tm, nk, _tile(tm, 128)),
        out_shape=(jax.ShapeDtypeStruct((t_all, d), F32),
                   jax.ShapeDtypeStruct((t_all, LANES), I32),
                   jax.ShapeDtypeStruct((t_all, LANES), F32),
                   jax.ShapeDtypeStruct((t_all, LANES), I32),
                   jax.ShapeDtypeStruct((1, LANES), I32)),
        grid=(t_all // tm, nk),
        in_specs=[pl.BlockSpec((tm, tk), lambda i, k: (rowp(i), attn_k(k))),
                  pl.BlockSpec((tm, tk), lambda i, k: (rows(i), attn_k(k))),
                  pl.BlockSpec((tm, tk), lambda i, k: (rowp(i), conv_k(k))),
                  pl.BlockSpec((tm, tk), lambda i, k: (rows(i), conv_k(k))),
                  pl.BlockSpec((tk, d), lambda i, k: (k, 0)),
                  any_spec, any_spec, vec, vec,
                  pl.BlockSpec((d, 2 * LANES), lambda i, k: (0, 0)),
                  pl.BlockSpec((1, LANES), lambda i, k: (0, 0))],
        out_specs=(pl.BlockSpec((tm, d), lambda i, k: (i, 0)), tok, tok, tok,
                   pl.BlockSpec((1, LANES), lambda i, k: (0, 0))),
        scratch_shapes=[pltpu.VMEM((tm, d), F32), pltpu.VMEM((tm, d), F32), pltpu.VMEM((tm, LANES), F32),
                        pltpu.VMEM((1, LANES), F32),
                        pltpu.SemaphoreType.DMA(())],
        compiler_params=_params(("arbitrary", "arbitrary")),
        name="outproj_ln_router",
    )(o_p, o_s, u_p, u_s, w_out_bf, xp, xs, ln_g, ln_b, rw_pad, rb_pad)


def _row_gather_start(tok_ref, r, src_hbm, dst_ref, sem):
    pltpu.make_async_copy(src_hbm.at[pl.ds(tok_ref[r], 1), :], dst_ref.at[pl.ds(r, 1), :], sem).start()


def _row_gather_wait(n, src_hbm, dst_ref, sem):
    pltpu.make_async_copy(src_hbm.at[pl.ds(0, n), :], dst_ref.at[pl.ds(0, n), :], sem).wait()


def _gate_up_kernel(bm, nj, be_ref, rows_ref, tok_ref, nxt_ref, h_hbm, wg_ref, wu_ref, bg_ref, bu_ref, o_ref,
                    xf_ref, xb_ref, sem):
    m = pl.program_id(0)
    j = pl.program_id(1)
    rows = rows_ref[m]
    active = rows > 0
    step_rows = bm // nj
    half = bm // 2

    @pl.when((m == 0) & (j == 0))
    def _first_block():
        def start(r, carry):
            _row_gather_start(tok_ref.at[0, 0], r, h_hbm, xf_ref, sem)
            return carry
        lax.fori_loop(0, bm, start, 0, unroll=8)

    @pl.when(j == 0)
    def _rows_ready():
        prev_active = rows_ref[jnp.maximum(m - 1, 0)] > 0

        @pl.when(active | prev_active)
        def _wait():
            _row_gather_wait(bm, h_hbm, xf_ref, sem)

        @pl.when(active)
        def _cast():
            xb_ref[...] = xf_ref[...].astype(BF16)

    def compute(n):
        for r in range(step_rows):
            _row_gather_start(nxt_ref.at[0, 0], j * step_rows + r, h_hbm, xf_ref, sem)
        xb = xb_ref[0:n, :]
        g = jnp.dot(xb, wg_ref[...], preferred_element_type=F32) + bg_ref[...]
        up = jnp.dot(xb, wu_ref[...], preferred_element_type=F32) + bu_ref[...]
        g = jnp.minimum(g, SWIGLU_LIMIT)
        up = jnp.clip(up, -SWIGLU_LIMIT, SWIGLU_LIMIT)
        o_ref[0:n, :] = ((up + 1.0) * (g * (1.0 / (1.0 + jnp.exp(-SWIGLU_ALPHA * g))))).astype(BF16)
        if n < bm:
            o_ref[n:, :] = jnp.zeros((bm - n, o_ref.shape[1]), BF16)

    pl.when(rows > half)(lambda: compute(bm))
    pl.when(active & (rows <= half))(lambda: compute(half))

    @pl.when(jnp.logical_not(active))
    def _idle():
        o_ref[...] = jnp.zeros_like(o_ref)


def _gate_up(block_e, block_rows, slot_tok, h, w_gu_bf, b_gu, bm, tn):
    nb = block_e.shape[0]
    d = h.shape[1]
    f = w_gu_bf.shape[2] // 2
    nj = f // tn
    col = lambda j, act, m: jnp.where(act[m] > 0, j, nj - 1)
    grid_spec = pltpu.PrefetchScalarGridSpec(
        num_scalar_prefetch=2,
        grid=(nb, nj),
        in_specs=[pl.BlockSpec((1, 1, bm), lambda m, j, be, act: (m, 0, 0), memory_space=pltpu.SMEM),
                  pl.BlockSpec((1, 1, bm), lambda m, j, be, act: (jnp.minimum(m + 1, nb - 1), 0, 0),
                               memory_space=pltpu.SMEM),
                  pl.BlockSpec(memory_space=pl.ANY),
                  pl.BlockSpec((None, d, tn), lambda m, j, be, act: (be[m], 0, col(j, act, m))),
                  pl.BlockSpec((None, d, tn), lambda m, j, be, act: (be[m], 0, nj + col(j, act, m))),
                  pl.BlockSpec((None, 1, tn), lambda m, j, be, act: (be[m], 0, col(j, act, m))),
                  pl.BlockSpec((None, 1, tn), lambda m, j, be, act: (be[m], 0, nj + col(j, act, m)))],
        out_specs=pl.BlockSpec((bm, tn), lambda m, j, be, act: (m, j)),
        scratch_shapes=[pltpu.VMEM((bm, d), F32), pltpu.VMEM((bm, d), BF16), pltpu.SemaphoreType.DMA(())],
    )
    return pl.pallas_call(
        functools.partial(_gate_up_kernel, bm, nj),
        out_shape=jax.ShapeDtypeStruct((nb * bm, f), BF16),
        grid_spec=grid_spec,
        compiler_params=_params(("arbitrary", "arbitrary")),
        name="moe_gate_up",
    )(block_e, block_rows, slot_tok, slot_tok, h, w_gu_bf, w_gu_bf, b_gu, b_gu)


def _down_kernel(be_ref, rows_ref, x_ref, w_ref, b_ref, o_ref):
    rows = rows_ref[pl.program_id(0)]
    active = rows > 0
    bm = x_ref.shape[0]
    half = bm // 2

    def compute(n):
        o_ref[0:n, :] = jnp.dot(x_ref[0:n, :], w_ref[...], preferred_element_type=F32) + b_ref[...]
        if n < bm:
            o_ref[n:, :] = jnp.zeros((bm - n, o_ref.shape[1]), F32)

    pl.when(rows > half)(lambda: compute(bm))
    pl.when(active & (rows <= half))(lambda: compute(half))

    @pl.when(jnp.logical_not(active))
    def _idle():
        o_ref[...] = jnp.zeros_like(o_ref)


def _down(block_e, block_rows, hmid, w_dn_bf, b_dn, bm, tn):
    nb = block_e.shape[0]
    f = hmid.shape[1]
    d = w_dn_bf.shape[2]
    nj = d // tn
    col = lambda j, act, m: jnp.where(act[m] > 0, j, nj - 1)
    grid_spec = pltpu.PrefetchScalarGridSpec(
        num_scalar_prefetch=2,
        grid=(nb, nj),
        in_specs=[pl.BlockSpec((bm, f), lambda m, j, be, act: (m, 0)),
                  pl.BlockSpec((None, f, tn), lambda m, j, be, act: (be[m], 0, col(j, act, m))),
                  pl.BlockSpec((None, 1, tn), lambda m, j, be, act: (be[m], 0, col(j, act, m)))],
        out_specs=pl.BlockSpec((bm, tn), lambda m, j, be, act: (m, j)),
    )
    return pl.pallas_call(
        _down_kernel,
        out_shape=jax.ShapeDtypeStruct((nb * bm, d), F32),
        grid_spec=grid_spec,
        compiler_params=_params(("arbitrary", "arbitrary")),
        name="moe_down",
    )(block_e, block_rows, hmid, w_dn_bf, b_dn)


def _combine_kernel(alpha, tc, pos_ref, nxt_ref, gate_ref, h_ref, ys_hbm, g_ref, b_ref, o_ref, buf_ref, sem):
    i = pl.program_id(0)
    slot = i % 2
    n_rows = TOP_K * tc

    def start_rows(tile_pos_ref, to):
        def start(r, carry):
            _row_gather_start(tile_pos_ref.at[0, 0], r, ys_hbm, buf_ref.at[to], sem.at[to])
            return carry
        lax.fori_loop(0, n_rows, start, 0, unroll=8)

    pl.when(i == 0)(lambda: start_rows(pos_ref, 0))
    pl.when(i + 1 < pl.num_programs(0))(lambda: start_rows(nxt_ref, 1 - slot))
    _row_gather_wait(n_rows, ys_hbm, buf_ref.at[slot], sem.at[slot])
    gates = gate_ref[...]
    f = jnp.zeros(h_ref.shape, F32)
    for j in range(TOP_K):
        f = f + gates[:, j:j + 1] * buf_ref[slot, j * tc:(j + 1) * tc, :]
    o_ref[...] = _layer_norm(alpha * h_ref[...] + f, g_ref[...], b_ref[...])


def _combine(pos_tiles, gates, h, ys, ln_g, ln_b, row0, n_rows, alpha, tc):
    d = h.shape[1]
    t0 = row0 // tc
    vec = pl.BlockSpec((1, d), lambda i: (0, 0))
    return pl.pallas_call(
        functools.partial(_combine_kernel, alpha, tc),
        out_shape=jax.ShapeDtypeStruct((n_rows, d), F32),
        grid=(n_rows // tc,),
        in_specs=[pl.BlockSpec((1, 1, TOP_K * tc), lambda i: (t0 + i, 0, 0), memory_space=pltpu.SMEM),
                  pl.BlockSpec((1, 1, TOP_K * tc), lambda i: (t0 + jnp.minimum(i + 1, n_rows // tc - 1), 0, 0),
                               memory_space=pltpu.SMEM),
                  pl.BlockSpec((tc, LANES), lambda i: (t0 + i, 0)),
                  pl.BlockSpec((tc, d), lambda i: (t0 + i, 0)),
                  pl.BlockSpec(memory_space=pl.ANY), vec, vec],
        out_specs=pl.BlockSpec((tc, d), lambda i: (i, 0)),
        scratch_shapes=[pltpu.VMEM((2, TOP_K * tc, d), F32), pltpu.SemaphoreType.DMA((2,))],
        compiler_params=_params(("arbitrary",)),
        name="moe_combine_ln",
    )(pos_tiles, pos_tiles, gates, h, ys, ln_g, ln_b)


def _rope_tables(seq_len):
    inv_freq = ROPE_THETA ** (-jnp.arange(0, ROT_DIM, 2, dtype=F32) / ROT_DIM)
    ang = jnp.arange(seq_len, dtype=F32)[:, None] * inv_freq[None, :]
    cos, sin = jnp.cos(ang), jnp.sin(ang)
    rest = HEAD_DIM - ROT_DIM
    cos_t = jnp.concatenate([cos, cos, jnp.ones((seq_len, rest), F32)], axis=1)
    sin_t = jnp.concatenate([-sin, sin, jnp.zeros((seq_len, rest), F32)], axis=1)
    return cos_t, sin_t


def _layer(xp, xs, shape_p, shape_s, lam_init, alpha, w_in, b_conv_in, conv_w, conv_b, conv_ln_g, conv_ln_b,
           lq1, lk1, lq2, lk2, subln_g, w_out, ln1_g, ln1_b, router_w, router_b, w_gu, b_gu, w_dn, b_dn, ln2_g, ln2_b):
    (bp, sp), (bs, ss) = shape_p, shape_s
    tp, d = xp.shape
    ts_ = xs.shape[0]
    t_all = tp + ts_
    d_attn = d // 2
    d_conv = d - d_attn
    n_heads = d_attn // HEAD_W
    n_exp = router_w.shape[1]
    row = lambda a: a.reshape(1, -1)

    tm = _tile(math.gcd(math.gcd(tp, ts_), math.gcd(sp, ss)), 512)
    tn = _tile(math.gcd(2 * d_attn, d_conv), 512)
    w_in_bf = w_in.astype(BF16)
    cos_t, sin_t = _rope_tables(max(sp, ss))
    qk = _inproj("qk", xp, xs, w_in_bf, 0, 2 * d_attn, tm, tn, (cos_t, sin_t, sp // tm, ss // tm), BF16)
    v = _inproj("v", xp, xs, w_in_bf, 2 * d_attn, d_attn, tm, tn, None, BF16)
    u = _inproj("glu", xp, xs, w_in_bf, 3 * d_attn, d_conv, tm, tn, row(b_conv_in), F32)

    lam_refs = (row(lq1), row(lk1), row(lq2), row(lk2))
    o_p = _attention(qk, v, lam_refs, row(subln_g), 0, bp, sp, n_heads, lam_init, _tile(sp, ATTN_QUERY_TILE))
    o_s = _attention(qk, v, lam_refs, row(subln_g), tp, bs, ss, n_heads, lam_init, _tile(ss, ATTN_QUERY_TILE))
    conv_args = (conv_w, row(conv_b), row(conv_ln_g), row(conv_ln_b))
    u_p = _conv(u, *conv_args, 0, bp, sp, _tile(math.gcd(sp, tp), 128))
    u_s = _conv(u, *conv_args, tp, bs, ss, _tile(math.gcd(ss, tp), 128))

    rw_f32 = jnp.zeros((d, LANES), F32).at[:, :n_exp].set(router_w)
    rw_hi = rw_f32.astype(BF16)
    rw_pad = jnp.concatenate([rw_hi, (rw_f32 - rw_hi.astype(F32)).astype(BF16)], axis=1)
    rb_pad = jnp.zeros((1, LANES), F32).at[0, :n_exp].set(router_b)
    h, top_idx, gates, rank, counts = _outproj(
        o_p, o_s, u_p, u_s, w_out.astype(BF16), xp, xs, row(ln1_g), row(ln1_b), rw_pad, rb_pad, n_exp, alpha,
        tm, _tile(d_attn, 512))

    bm = _tile(t_all, 512)
    n_assign = t_all * TOP_K
    nb = n_assign // bm + n_exp
    cnt = counts[0, :n_exp]
    blocks = (cnt + bm - 1) // bm
    blocks_end = jnp.cumsum(blocks)
    first_slot = (blocks_end - blocks) * bm
    flat_e = top_idx[:, :TOP_K].reshape(-1)
    pos = jnp.take(first_slot, flat_e) + rank[:, :TOP_K].reshape(-1)
    slot_tok = jnp.zeros((nb * bm,), I32).at[pos].set(jnp.arange(n_assign, dtype=I32) // TOP_K, unique_indices=True)
    block_ids = jnp.arange(nb, dtype=I32)
    block_e = jnp.minimum(jnp.sum((block_ids[:, None] >= blocks_end[None, :]).astype(I32), axis=1), n_exp - 1)
    block_rows = jnp.take(cnt, block_e) - (block_ids - jnp.take(blocks_end - blocks, block_e)) * bm
    block_rows = jnp.where(block_ids < blocks_end[-1], jnp.clip(block_rows, 0, bm), 0).astype(I32)

    tf = _tile(w_dn.shape[1], 1024)
    hmid = _gate_up(block_e, block_rows, slot_tok.reshape(nb, 1, bm), h, w_gu.astype(BF16),
                    b_gu.reshape(n_exp, 1, -1), bm, tf)
    ys = _down(block_e, block_rows, hmid, w_dn.astype(BF16), b_dn.reshape(n_exp, 1, -1), bm, _tile(d, 2048))

    tc = _tile(math.gcd(tp, ts_), 256)
    pos_tiles = pos.reshape(t_all // tc, tc, TOP_K).transpose(0, 2, 1).reshape(t_all // tc, 1, TOP_K * tc)
    y_p = _combine(pos_tiles, gates, h, ys, row(ln2_g), row(ln2_b), 0, tp, alpha, tc)
    y_s = _combine(pos_tiles, gates, h, ys, row(ln2_g), row(ln2_b), tp, ts_, alpha, tc)
    return y_p, y_s


def kernel(x_prompt, x_sample, w_in, b_conv_in, conv_w, conv_b, conv_ln_g, conv_ln_b, lambda_q1, lambda_k1,
           lambda_q2, lambda_k2, subln_g, w_out, ln1_g, ln1_b, router_w, router_b, w_gate_up, b_gate_up, w_down,
           b_down, ln2_g, ln2_b):
    bp, sp, d = x_prompt.shape
    bs, ss, _ = x_sample.shape
    depth = w_in.shape[0]
    alpha = (2.0 * depth) ** 0.25
    xp = x_prompt.reshape(bp * sp, d)
    xs = x_sample.reshape(bs * ss, d)
    for l in range(depth):
        lam_init = 0.8 - 0.6 * math.exp(-0.3 * l)
        xp, xs = _layer(xp, xs, (bp, sp), (bs, ss), lam_init, alpha, w_in[l], b_conv_in[l], conv_w[l], conv_b[l],
                        conv_ln_g[l], conv_ln_b[l], lambda_q1[l], lambda_k1[l], lambda_q2[l], lambda_k2[l],
                        subln_g[l], w_out[l], ln1_g[l], ln1_b[l], router_w[l], router_b[l], w_gate_up[l],
                        b_gate_up[l], w_down[l], b_down[l], ln2_g[l], ln2_b[l])
    return xp.reshape(bp, sp, d), xs.reshape(bs, ss, d)
```

```python
import functools
import math

import jax
import jax.numpy as jnp
from jax import lax
from jax.experimental import pallas as pl
from jax.experimental.pallas import tpu as pltpu

F32 = jnp.float32
BF16 = jnp.bfloat16
I32 = jnp.int32

LANES = 128
SUBLANES = 8
HALO = 16
HEAD_DIM = 128
HEAD_W = 2 * HEAD_DIM
ROT_DIM = HEAD_DIM // 4
ROT_HALF = ROT_DIM // 2
ROPE_THETA = 500000.0
CONV_SIZE = 31
CONV_PAD = (CONV_SIZE - 1) // 2
TOP_K = 4
SWIGLU_LIMIT = 7.0
SWIGLU_ALPHA = 1.702
LN_EPS = 1e-5
LOG2E = 1.4426950408889634
VMEM_LIMIT = 56 * 1024 * 1024
ATTN_QUERY_TILE = 512
ATTN_SCORE_BYTES = 24 * 1024 * 1024
ATTN_KEY_CHUNK = 512


def _tile(dim, pref):
    t = pref
    while t > 1 and dim % t:
        t //= 2
    return t


def _params(sem, vmem=VMEM_LIMIT):
    return pltpu.CompilerParams(dimension_semantics=sem, vmem_limit_bytes=vmem)


def _layer_norm(z, g, b):
    mu = jnp.mean(z, axis=-1, keepdims=True)
    zc = z - mu
    var = jnp.mean(zc * zc, axis=-1, keepdims=True)
    return zc * lax.rsqrt(var + LN_EPS) * g + b


def _row_copy(i, n_p, tm, xp_hbm, xs_hbm, dst, sem):
    cp_p = pltpu.make_async_copy(xp_hbm.at[pl.ds(jnp.minimum(i, n_p - 1) * tm, tm), :], dst, sem)
    cp_s = pltpu.make_async_copy(xs_hbm.at[pl.ds(jnp.maximum(i - n_p, 0) * tm, tm), :], dst, sem)
    return cp_p, cp_s


def _row_start(i, n_p, tm, xp_hbm, xs_hbm, dst, sem):
    cp_p, cp_s = _row_copy(i, n_p, tm, xp_hbm, xs_hbm, dst, sem)
    pl.when(i < n_p)(cp_p.start)
    pl.when(i >= n_p)(cp_s.start)


def _row_wait(i, n_p, tm, xp_hbm, xs_hbm, dst, sem):
    cp_p, cp_s = _row_copy(i, n_p, tm, xp_hbm, xs_hbm, dst, sem)
    pl.when(i < n_p)(cp_p.wait)
    pl.when(i >= n_p)(cp_s.wait)


def _inproj_kernel(mode, n_p, tm, tn, xp_hbm, xs_hbm, *refs):
    if mode == "qk":
        w_ref, cos_ref, sin_ref, o_ref, xf_ref, xb_ref, sem = refs
    elif mode == "v":
        w_ref, o_ref, xf_ref, xb_ref, sem = refs
    else:
        wa_ref, wg_ref, ba_ref, bg_ref, o_ref, xf_ref, xb_ref, sem = refs
    i = pl.program_id(0)
    j = pl.program_id(1)

    @pl.when(j == 0)
    def _load_rows():
        slot = i % 2

        @pl.when(i == 0)
        def _first():
            _row_start(i, n_p, tm, xp_hbm, xs_hbm, xf_ref.at[0], sem.at[0])

        _row_wait(i, n_p, tm, xp_hbm, xs_hbm, xf_ref.at[slot], sem.at[slot])

        @pl.when(i + 1 < pl.num_programs(0))
        def _next():
            _row_start(i + 1, n_p, tm, xp_hbm, xs_hbm, xf_ref.at[1 - slot], sem.at[1 - slot])

        xb_ref[...] = xf_ref[slot].astype(BF16)

    xb = xb_ref[...]
    if mode == "qk":
        acc = jnp.dot(xb, w_ref[...], preferred_element_type=F32)
        c = cos_ref[...]
        s = sin_ref[...]
        lane = lax.broadcasted_iota(I32, (tm, LANES), 1)
        for ch in range(tn // LANES):
            t = acc[:, ch * LANES:(ch + 1) * LANES]
            partner = jnp.where(lane < ROT_HALF, pltpu.roll(t, LANES - ROT_HALF, 1), pltpu.roll(t, ROT_HALF, 1))
            o_ref[:, ch * LANES:(ch + 1) * LANES] = (t * c + partner * s).astype(BF16)
    elif mode == "v":
        o_ref[...] = jnp.dot(xb, w_ref[...], preferred_element_type=F32).astype(BF16)
    else:
        a = jnp.dot(xb, wa_ref[...], preferred_element_type=F32) + ba_ref[...]
        g = jnp.dot(xb, wg_ref[...], preferred_element_type=F32) + bg_ref[...]
        o_ref[...] = a * (1.0 / (1.0 + jnp.exp(-g)))


def _inproj(mode, xp, xs, w_bf, col0, ncols, tm, tn, extra, out_dtype):
    tp, d = xp.shape
    t_all = tp + xs.shape[0]
    n_p = tp // tm
    cb0 = col0 // tn
    any_spec = pl.BlockSpec(memory_space=pl.ANY)
    w_spec = pl.BlockSpec((d, tn), lambda i, j: (0, cb0 + j))
    if mode == "qk":
        cos_t, sin_t, sp_t, ss_t = extra
        pos = lambda i, j: (jnp.where(i < n_p, i % sp_t, (i - n_p) % ss_t), 0)
        in_specs = [any_spec, any_spec, w_spec, pl.BlockSpec((tm, LANES), pos), pl.BlockSpec((tm, LANES), pos)]
        args = (xp, xs, w_bf, cos_t, sin_t)
    elif mode == "v":
        in_specs = [any_spec, any_spec, w_spec]
        args = (xp, xs, w_bf)
    else:
        bias = extra
        gb0 = (col0 + ncols) // tn
        in_specs = [any_spec, any_spec, w_spec, pl.BlockSpec((d, tn), lambda i, j: (0, gb0 + j)),
                    pl.BlockSpec((1, tn), lambda i, j: (0, j)),
                    pl.BlockSpec((1, tn), lambda i, j: (0, ncols // tn + j))]
        args = (xp, xs, w_bf, w_bf, bias, bias)
    return pl.pallas_call(
        functools.partial(_inproj_kernel, mode, n_p, tm, tn),
        out_shape=jax.ShapeDtypeStruct((t_all, ncols), out_dtype),
        grid=(t_all // tm, ncols // tn),
        in_specs=in_specs,
        out_specs=pl.BlockSpec((tm, tn), lambda i, j: (i, j)),
        scratch_shapes=[pltpu.VMEM((2, tm, d), F32), pltpu.VMEM((tm, d), BF16), pltpu.SemaphoreType.DMA((2,))],
        compiler_params=_params(("arbitrary", "arbitrary")),
        name="inproj_" + mode,
    )(*args)


def _lane_fold(x, op):
    parts = [x[:, t * LANES:(t + 1) * LANES] for t in range(x.shape[1] // LANES)]
    while len(parts) > 1:
        parts = [op(parts[t], parts[t + 1]) for t in range(0, len(parts) - 1, 2)] + parts[len(parts) & ~1:]
    return parts[0]


def _attn_kernel(lam_init, tk, q_ref, k_ref, v_ref, lq1_ref, lk1_ref, lq2_ref, lk2_ref, g_ref, o_ref,
                 s_ref, p_ref, fold_ref, acc_ref):
    tq = q_ref.shape[0]
    c = (HEAD_DIM ** -0.5) * LOG2E
    n_chunks = k_ref.shape[0] // tk
    half = n_chunks // 2

    def scores(mp):
        cols = slice(mp * HEAD_DIM, (mp + 1) * HEAD_DIM)
        q = q_ref[:, cols]
        fold = None
        for ch in range(n_chunks):
            keys = slice(ch * tk, (ch + 1) * tk)
            s = lax.dot_general(q, k_ref[keys, cols], (((1,), (1,)), ((), ())), preferred_element_type=F32)
            s_ref[mp, :, keys] = s
            part = _lane_fold(s, jnp.maximum)
            fold = part if fold is None else jnp.maximum(fold, part)
        fold_ref[mp] = fold

    def probs(mp, lo, hi):
        row_max = jnp.max(fold_ref[mp], axis=-1, keepdims=True)
        fold = None if lo == 0 else fold_ref[2 + mp]
        for ch in range(lo, hi):
            keys = slice(ch * tk, (ch + 1) * tk)
            p = jnp.exp2((s_ref[mp, :, keys] - row_max) * c)
            part = _lane_fold(p, jnp.add)
            fold = part if fold is None else fold + part
            p_ref[mp * tq:(mp + 1) * tq, keys] = p.astype(BF16)
        fold_ref[2 + mp] = fold

    def values(lo, hi):
        acc = None
        for ch in range(lo, hi):
            keys = slice(ch * tk, (ch + 1) * tk)
            pv = jnp.dot(p_ref[:, keys], v_ref[keys, :], preferred_element_type=F32)
            acc = pv if acc is None else acc + pv
        return acc

    scores(0)
    scores(1)
    probs(0, 0, n_chunks)
    probs(1, 0, half)
    acc_ref[...] = values(0, half)
    probs(1, half, n_chunks)
    acc = values(half, n_chunks) + acc_ref[...]
    l0 = jnp.sum(fold_ref[2], axis=-1, keepdims=True)
    l1 = jnp.sum(fold_ref[3], axis=-1, keepdims=True)
    lam = (jnp.exp(jnp.sum(lq1_ref[...] * lk1_ref[...], axis=-1, keepdims=True))
           - jnp.exp(jnp.sum(lq2_ref[...] * lk2_ref[...], axis=-1, keepdims=True)) + lam_init)
    o = acc[:tq] * (1.0 / l0) - acc[tq:] * (lam / l1)
    ms = jnp.mean(o * o, axis=-1, keepdims=True)
    o_ref[...] = (o * lax.rsqrt(ms + LN_EPS) * g_ref[...] * (1.0 - lam_init)).astype(BF16)


def _attention(qk, v, lam_refs, subln_g, row0, batch, seq, n_heads, lam_init, tq):
    assert row0 % seq == 0 and seq % tq == 0
    qt = seq // tq
    kcol0 = n_heads
    small = pl.BlockSpec((1, HEAD_DIM), lambda b, h, i: (0, 0))
    return pl.pallas_call(
        functools.partial(_attn_kernel, lam_init, _tile(seq // 2, ATTN_KEY_CHUNK)),
        out_shape=jax.ShapeDtypeStruct((batch * seq, n_heads * HEAD_W), BF16),
        grid=(batch, n_heads, qt),
        in_specs=[pl.BlockSpec((tq, HEAD_W), lambda b, h, i: (row0 // tq + b * qt + i, h)),
                  pl.BlockSpec((seq, HEAD_W), lambda b, h, i: (row0 // seq + b, kcol0 + h)),
                  pl.BlockSpec((seq, HEAD_W), lambda b, h, i: (row0 // seq + b, h)),
                  small, small, small, small,
                  pl.BlockSpec((1, HEAD_W), lambda b, h, i: (0, 0))],
        out_specs=pl.BlockSpec((tq, HEAD_W), lambda b, h, i: (b * qt + i, h)),
        scratch_shapes=[pltpu.VMEM((2, tq, seq), F32), pltpu.VMEM((2 * tq, seq), BF16),
                        pltpu.VMEM((4, tq, LANES), F32), pltpu.VMEM((2 * tq, HEAD_W), F32)],
        compiler_params=_params(("arbitrary", "arbitrary", "arbitrary")),
        name="diff_attention",
    )(qk, qk, v, *lam_refs, subln_g)


def _conv_kernel(ts, nt, rc, prev_ref, x_ref, next_ref, w_ref, cb_ref, g_ref, b_ref, o_ref, pad_ref, sh_ref, u_ref):
    t = pl.program_id(1)
    c_all = x_ref.shape[1]
    pad_ref[0:HALO, :] = jnp.where(t > 0, prev_ref[...], 0.0)
    pad_ref[HALO:HALO + ts, :] = x_ref[...]
    pad_ref[HALO + ts:, :] = jnp.where(t < nt - 1, next_ref[...], 0.0)
    span = sh_ref.shape[1]
    for s in range(1, SUBLANES):
        sh_ref[s - 1] = pad_ref[s:s + span, :]
    off = HALO - CONV_PAD
    for cc in range(c_all // LANES):
        cs = slice(cc * LANES, (cc + 1) * LANES)
        for r in range(ts // rc):
            acc = jnp.zeros((rc, LANES), F32)
            for k in range(CONV_SIZE):
                up, s = divmod(k + off, SUBLANES)
                lo = r * rc + up * SUBLANES
                rows = pad_ref[lo:lo + rc, cs] if s == 0 else sh_ref[s - 1, lo:lo + rc, cs]
                acc = acc + rows * w_ref[k:k + 1, cs]
            u_ref[r * rc:(r + 1) * rc, cs] = acc
    u = _layer_norm(u_ref[...] + cb_ref[...], g_ref[...], b_ref[...])
    o_ref[...] = (u * (1.0 / (1.0 + jnp.exp(-u)))).astype(BF16)


def _conv(u, conv_w, conv_b, ln_g, ln_b, row0, batch, seq, ts):
    c = u.shape[1]
    nt = seq // ts
    assert row0 % ts == 0 and seq % ts == 0 and ts % HALO == 0
    hb = ts // HALO
    last_h = u.shape[0] // HALO - 1
    base = lambda b, t: row0 // ts + b * nt + t
    vec = pl.BlockSpec((1, c), lambda b, t: (0, 0))
    return pl.pallas_call(
        functools.partial(_conv_kernel, ts, nt, _tile(ts, 32)),
        out_shape=jax.ShapeDtypeStruct((batch * seq, c), BF16),
        grid=(batch, nt),
        in_specs=[pl.BlockSpec((HALO, c), lambda b, t: (jnp.maximum(base(b, t) * hb - 1, 0), 0)),
                  pl.BlockSpec((ts, c), lambda b, t: (base(b, t), 0)),
                  pl.BlockSpec((HALO, c), lambda b, t: (jnp.minimum((base(b, t) + 1) * hb, last_h), 0)),
                  pl.BlockSpec((CONV_SIZE, c), lambda b, t: (0, 0)),
                  vec, vec, vec],
        out_specs=pl.BlockSpec((ts, c), lambda b, t: (b * nt + t, 0)),
        scratch_shapes=[pltpu.VMEM((ts + 2 * HALO, c), F32),
                        pltpu.VMEM((SUBLANES - 1, ts + 2 * HALO - SUBLANES, c), F32),
                        pltpu.VMEM((ts, c), F32)],
        compiler_params=_params(("arbitrary", "arbitrary")),
        name="conformer_conv",
    )(u, u, u, conv_w, conv_b, ln_g, ln_b)


def _outproj_kernel(alpha, n_exp, n_p, tm, nk, rc, *refs):
    (op_ref, os_ref, up_ref, us_ref, w_ref, xp_hbm, xs_hbm, g_ref, b_ref, rw_ref, rb_ref,
     h_ref, idx_ref, gate_ref, rank_ref, cnt_ref, acc_ref, x_ref, logit_ref, run_ref, sem) = refs
    i = pl.program_id(0)
    k = pl.program_id(1)
    half = nk // 2

    @pl.when(k == 0)
    def _init():
        acc_ref[...] = jnp.zeros_like(acc_ref)
        _row_start(i, n_p, tm, xp_hbm, xs_hbm, x_ref, sem)

    @pl.when((i == 0) & (k == 0))
    def _init_counts():
        run_ref[...] = jnp.zeros_like(run_ref)

    for src, (lo, is_p) in zip((op_ref, up_ref, os_ref, us_ref), ((0, True), (half, True), (0, False), (half, False))):
        rows = (i < n_p) if is_p else (i >= n_p)

        @pl.when(rows & (k >= lo) & (k < lo + half))
        def _acc(src=src):
            acc_ref[...] += jnp.dot(src[...], w_ref[...], preferred_element_type=F32)

    @pl.when(k == nk - 1)
    def _epilogue():
        _row_wait(i, n_p, tm, xp_hbm, xs_hbm, x_ref, sem)
        for r0 in range(0, tm, rc):
            h = _layer_norm(alpha * x_ref[r0:r0 + rc, :] + acc_ref[r0:r0 + rc, :], g_ref[...], b_ref[...])
            h_ref[r0:r0 + rc, :] = h
            h_hi = h.astype(BF16)
            h_lo = (h - h_hi.astype(F32)).astype(BF16)
            big = jnp.dot(h_hi, rw_ref[...], preferred_element_type=F32)
            small = jnp.dot(h_lo, rw_ref[...], preferred_element_type=F32)
            logit_ref[r0:r0 + rc, :] = big[:, :LANES] + (big[:, LANES:] + small[:, :LANES])
        lane = lax.broadcasted_iota(I32, (tm, LANES), 1)
        lane_f = lane.astype(F32)
        logits = jnp.where(lane < n_exp, logit_ref[...] + rb_ref[...], -jnp.inf)
        idx_out = jnp.zeros((tm, LANES), F32)
        val_out = jnp.zeros((tm, LANES), F32)
        member = jnp.zeros((tm, LANES), F32)
        picks = []
        top = None
        for j in range(TOP_K):
            mx = jnp.max(logits, axis=-1, keepdims=True)
            top = mx if top is None else top
            pick = jnp.min(jnp.where(logits == mx, lane_f, float(LANES)), axis=-1, keepdims=True)
            hit = lane_f == pick
            picks.append(hit)
            idx_out = jnp.where(lane == j, pick, idx_out)
            val_out = jnp.where(lane == j, mx, val_out)
            member = jnp.where(hit, 1.0, member)
            logits = jnp.where(hit, -jnp.inf, logits)
        e = jnp.where(lane < TOP_K, jnp.exp(val_out - top), 0.0)
        gate_ref[...] = e * (1.0 / jnp.sum(e, axis=-1, keepdims=True))
        idx_ref[...] = idx_out.astype(I32)
        row = lax.broadcasted_iota(I32, (tm, tm), 0)
        col = lax.broadcasted_iota(I32, (tm, tm), 1)
        before = jnp.where(col < row, 1.0, 0.0).astype(BF16)
        prior = jnp.dot(before, member.astype(BF16), preferred_element_type=F32) + run_ref[...]
        rank_out = jnp.zeros((tm, LANES), F32)
        for j in range(TOP_K):
            rj = jnp.sum(jnp.where(picks[j], prior, 0.0), axis=-1, keepdims=True)
            rank_out = jnp.where(lane == j, rj, rank_out)
        rank_ref[...] = rank_out.astype(I32)
        run_ref[...] += jnp.sum(member, axis=0, keepdims=True)
        cnt_ref[...] = run_ref[...].astype(I32)


def _outproj(o_p, o_s, u_p, u_s, w_out_bf, xp, xs, ln_g, ln_b, rw_pad, rb_pad, n_exp, alpha, tm, tk):
    tp, d = xp.shape
    t_all = tp + xs.shape[0]
    n_p = tp // tm
    n_s = xs.shape[0] // tm
    half = o_p.shape[1] // tk
    nk = 2 * half
    rowp = lambda i: jnp.minimum(i, n_p - 1)
    rows = lambda i: jnp.clip(i - n_p, 0, n_s - 1)
    attn_k = lambda k: jnp.minimum(k, half - 1)
    conv_k = lambda k: jnp.clip(k - half, 0, half - 1)
    any_spec = pl.BlockSpec(memory_space=pl.ANY)
    vec = pl.BlockSpec((1, d), lambda i, k: (0, 0))
    tok = pl.BlockSpec((tm, LANES), lambda i, k: (i, 0))
    return pl.pallas_call(
        functools.partial(_outproj_kernel, alpha, n_exp, n_p, tm, nk, _tile(tm, 128)),
        out_shape=(jax.ShapeDtypeStruct((t_all, d), F32),
                   jax.ShapeDtypeStruct((t_all, LANES), I32),
                   jax.ShapeDtypeStruct((t_all, LANES), F32),
                   jax.ShapeDtypeStruct((t_all, LANES), I32),
                   jax.ShapeDtypeStruct((1, LANES), I32)),
        grid=(t_all // tm, nk),
        in_specs=[pl.BlockSpec((tm, tk), lambda i, k: (rowp(i), attn_k(k))),
                  pl.BlockSpec((tm, tk), lambda i, k: (rows(i), attn_k(k))),
                  pl.BlockSpec((tm, tk), lambda i, k: (rowp(i), conv_k(k))),
                  pl.BlockSpec((tm, tk), lambda i, k: (rows(i), conv_k(k))),
                  pl.BlockSpec((tk, d), lambda i, k: (k, 0)),
                  any_spec, any_spec, vec, vec,
                  pl.BlockSpec((d, 2 * LANES), lambda i, k: (0, 0)),
                  pl.BlockSpec((1, LANES), lambda i, k: (0, 0))],
        out_specs=(pl.BlockSpec((tm, d), lambda i, k: (i, 0)), tok, tok, tok,
                   pl.BlockSpec((1, LANES), lambda i, k: (0, 0))),
        scratch_shapes=[pltpu.VMEM((tm, d), F32), pltpu.VMEM((tm, d), F32), pltpu.VMEM((tm, LANES), F32),
                        pltpu.VMEM((1, LANES), F32),
                        pltpu.SemaphoreType.DMA(())],
        compiler_params=_params(("arbitrary", "arbitrary")),
        name="outproj_ln_router",
    )(o_p, o_s, u_p, u_s, w_out_bf, xp, xs, ln_g, ln_b, rw_pad, rb_pad)


def _row_gather_start(tok_ref, r, src_hbm, dst_ref, sem):
    pltpu.make_async_copy(src_hbm.at[pl.ds(tok_ref[r], 1), :], dst_ref.at[pl.ds(r, 1), :], sem).start()


def _row_gather_wait(n, src_hbm, dst_ref, sem):
    pltpu.make_async_copy(src_hbm.at[pl.ds(0, n), :], dst_ref.at[pl.ds(0, n), :], sem).wait()


def _gate_up_kernel(bm, nj, be_ref, rows_ref, tok_ref, nxt_ref, h_hbm, wg_ref, wu_ref, bg_ref, bu_ref, o_ref,
                    xf_ref, xb_ref, sem):
    m = pl.program_id(0)
    j = pl.program_id(1)
    rows = rows_ref[m]
    active = rows > 0
    step_rows = bm // nj
    half = bm // 2

    @pl.when((m == 0) & (j == 0))
    def _first_block():
        def start(r, carry):
            _row_gather_start(tok_ref.at[0, 0], r, h_hbm, xf_ref, sem)
            return carry
        lax.fori_loop(0, bm, start, 0, unroll=8)

    @pl.when(j == 0)
    def _rows_ready():
        prev_active = rows_ref[jnp.maximum(m - 1, 0)] > 0

        @pl.when(active | prev_active)
        def _wait():
            _row_gather_wait(bm, h_hbm, xf_ref, sem)

        @pl.when(active)
        def _cast():
            xb_ref[...] = xf_ref[...].astype(BF16)

    def compute(n):
        for r in range(step_rows):
            _row_gather_start(nxt_ref.at[0, 0], j * step_rows + r, h_hbm, xf_ref, sem)
        xb = xb_ref[0:n, :]
        g = jnp.dot(xb, wg_ref[...], preferred_element_type=F32) + bg_ref[...]
        up = jnp.dot(xb, wu_ref[...], preferred_element_type=F32) + bu_ref[...]
        g = jnp.minimum(g, SWIGLU_LIMIT)
        up = jnp.clip(up, -SWIGLU_LIMIT, SWIGLU_LIMIT)
        o_ref[0:n, :] = ((up + 1.0) * (g * (1.0 / (1.0 + jnp.exp(-SWIGLU_ALPHA * g))))).astype(BF16)
        if n < bm:
            o_ref[n:, :] = jnp.zeros((bm - n, o_ref.shape[1]), BF16)

    pl.when(rows > half)(lambda: compute(bm))
    pl.when(active & (rows <= half))(lambda: compute(half))

    @pl.when(jnp.logical_not(active))
    def _idle():
        o_ref[...] = jnp.zeros_like(o_ref)


def _gate_up(block_e, block_rows, slot_tok, h, w_gu_bf, b_gu, bm, tn):
    nb = block_e.shape[0]
    d = h.shape[1]
    f = w_gu_bf.shape[2] // 2
    nj = f // tn
    col = lambda j, act, m: jnp.where(act[m] > 0, j, nj - 1)
    grid_spec = pltpu.PrefetchScalarGridSpec(
        num_scalar_prefetch=2,
        grid=(nb, nj),
        in_specs=[pl.BlockSpec((1, 1, bm), lambda m, j, be, act: (m, 0, 0), memory_space=pltpu.SMEM),
                  pl.BlockSpec((1, 1, bm), lambda m, j, be, act: (jnp.minimum(m + 1, nb - 1), 0, 0),
                               memory_space=pltpu.SMEM),
                  pl.BlockSpec(memory_space=pl.ANY),
                  pl.BlockSpec((None, d, tn), lambda m, j, be, act: (be[m], 0, col(j, act, m))),
                  pl.BlockSpec((None, d, tn), lambda m, j, be, act: (be[m], 0, nj + col(j, act, m))),
                  pl.BlockSpec((None, 1, tn), lambda m, j, be, act: (be[m], 0, col(j, act, m))),
                  pl.BlockSpec((None, 1, tn), lambda m, j, be, act: (be[m], 0, nj + col(j, act, m)))],
        out_specs=pl.BlockSpec((bm, tn), lambda m, j, be, act: (m, j)),
        scratch_shapes=[pltpu.VMEM((bm, d), F32), pltpu.VMEM((bm, d), BF16), pltpu.SemaphoreType.DMA(())],
    )
    return pl.pallas_call(
        functools.partial(_gate_up_kernel, bm, nj),
        out_shape=jax.ShapeDtypeStruct((nb * bm, f), BF16),
        grid_spec=grid_spec,
        compiler_params=_params(("arbitrary", "arbitrary")),
        name="moe_gate_up",
    )(block_e, block_rows, slot_tok, slot_tok, h, w_gu_bf, w_gu_bf, b_gu, b_gu)


def _down_kernel(be_ref, rows_ref, x_ref, w_ref, b_ref, o_ref):
    rows = rows_ref[pl.program_id(0)]
    active = rows > 0
    bm = x_ref.shape[0]
    half = bm // 2

    def compute(n):
        o_ref[0:n, :] = jnp.dot(x_ref[0:n, :], w_ref[...], preferred_element_type=F32) + b_ref[...]
        if n < bm:
            o_ref[n:, :] = jnp.zeros((bm - n, o_ref.shape[1]), F32)

    pl.when(rows > half)(lambda: compute(bm))
    pl.when(active & (rows <= half))(lambda: compute(half))

    @pl.when(jnp.logical_not(active))
    def _idle():
        o_ref[...] = jnp.zeros_like(o_ref)


def _down(block_e, block_rows, hmid, w_dn_bf, b_dn, bm, tn):
    nb = block_e.shape[0]
    f = hmid.shape[1]
    d = w_dn_bf.shape[2]
    nj = d // tn
    col = lambda j, act, m: jnp.where(act[m] > 0, j, nj - 1)
    grid_spec = pltpu.PrefetchScalarGridSpec(
        num_scalar_prefetch=2,
        grid=(nb, nj),
        in_specs=[pl.BlockSpec((bm, f), lambda m, j, be, act: (m, 0)),
                  pl.BlockSpec((None, f, tn), lambda m, j, be, act: (be[m], 0, col(j, act, m))),
                  pl.BlockSpec((None, 1, tn), lambda m, j, be, act: (be[m], 0, col(j, act, m)))],
        out_specs=pl.BlockSpec((bm, tn), lambda m, j, be, act: (m, j)),
    )
    return pl.pallas_call(
        _down_kernel,
        out_shape=jax.ShapeDtypeStruct((nb * bm, d), F32),
        grid_spec=grid_spec,
        compiler_params=_params(("arbitrary", "arbitrary")),
        name="moe_down",
    )(block_e, block_rows, hmid, w_dn_bf, b_dn)


def _combine_kernel(alpha, tc, pos_ref, nxt_ref, gate_ref, h_ref, ys_hbm, g_ref, b_ref, o_ref, buf_ref, sem):
    i = pl.program_id(0)
    slot = i % 2
    n_rows = TOP_K * tc

    def start_rows(tile_pos_ref, to):
        def start(r, carry):
            _row_gather_start(tile_pos_ref.at[0, 0], r, ys_hbm, buf_ref.at[to], sem.at[to])
            return carry
        lax.fori_loop(0, n_rows, start, 0, unroll=8)

    pl.when(i == 0)(lambda: start_rows(pos_ref, 0))
    pl.when(i + 1 < pl.num_programs(0))(lambda: start_rows(nxt_ref, 1 - slot))
    _row_gather_wait(n_rows, ys_hbm, buf_ref.at[slot], sem.at[slot])
    gates = gate_ref[...]
    f = jnp.zeros(h_ref.shape, F32)
    for j in range(TOP_K):
        f = f + gates[:, j:j + 1] * buf_ref[slot, j * tc:(j + 1) * tc, :]
    o_ref[...] = _layer_norm(alpha * h_ref[...] + f, g_ref[...], b_ref[...])


def _combine(pos_tiles, gates, h, ys, ln_g, ln_b, row0, n_rows, alpha, tc):
    d = h.shape[1]
    t0 = row0 // tc
    vec = pl.BlockSpec((1, d), lambda i: (0, 0))
    return pl.pallas_call(
        functools.partial(_combine_kernel, alpha, tc),
        out_shape=jax.ShapeDtypeStruct((n_rows, d), F32),
        grid=(n_rows // tc,),
        in_specs=[pl.BlockSpec((1, 1, TOP_K * tc), lambda i: (t0 + i, 0, 0), memory_space=pltpu.SMEM),
                  pl.BlockSpec((1, 1, TOP_K * tc), lambda i: (t0 + jnp.minimum(i + 1, n_rows // tc - 1), 0, 0),
                               memory_space=pltpu.SMEM),
                  pl.BlockSpec((tc, LANES), lambda i: (t0 + i, 0)),
                  pl.BlockSpec((tc, d), lambda i: (t0 + i, 0)),
                  pl.BlockSpec(memory_space=pl.ANY), vec, vec],
        out_specs=pl.BlockSpec((tc, d), lambda i: (i, 0)),
        scratch_shapes=[pltpu.VMEM((2, TOP_K * tc, d), F32), pltpu.SemaphoreType.DMA((2,))],
        compiler_params=_params(("arbitrary",)),
        name="moe_combine_ln",
    )(pos_tiles, pos_tiles, gates, h, ys, ln_g, ln_b)


def _rope_tables(seq_len):
    inv_freq = ROPE_THETA ** (-jnp.arange(0, ROT_DIM, 2, dtype=F32) / ROT_DIM)
    ang = jnp.arange(seq_len, dtype=F32)[:, None] * inv_freq[None, :]
    cos, sin = jnp.cos(ang), jnp.sin(ang)
    rest = HEAD_DIM - ROT_DIM
    cos_t = jnp.concatenate([cos, cos, jnp.ones((seq_len, rest), F32)], axis=1)
    sin_t = jnp.concatenate([-sin, sin, jnp.zeros((seq_len, rest), F32)], axis=1)
    return cos_t, sin_t


def _layer(xp, xs, shape_p, shape_s, lam_init, alpha, w_in, b_conv_in, conv_w, conv_b, conv_ln_g, conv_ln_b,
           lq1, lk1, lq2, lk2, subln_g, w_out, ln1_g, ln1_b, router_w, router_b, w_gu, b_gu, w_dn, b_dn, ln2_g, ln2_b):
    (bp, sp), (bs, ss) = shape_p, shape_s
    tp, d = xp.shape
    ts_ = xs.shape[0]
    t_all = tp + ts_
    d_attn = d // 2
    d_conv = d - d_attn
    n_heads = d_attn // HEAD_W
    n_exp = router_w.shape[1]
    row = lambda a: a.reshape(1, -1)

    tm = _tile(math.gcd(math.gcd(tp, ts_), math.gcd(sp, ss)), 512)
    tn = _tile(math.gcd(2 * d_attn, d_conv), 512)
    w_in_bf = w_in.astype(BF16)
    cos_t, sin_t = _rope_tables(max(sp, ss))
    tn_wide = _tile(d_attn, 2 * tn)
    qk = _inproj("qk", xp, xs, w_in_bf, 0, 2 * d_attn, tm, tn_wide, (cos_t, sin_t, sp // tm, ss // tm), BF16)
    v = _inproj("v", xp, xs, w_in_bf, 2 * d_attn, d_attn, tm, tn_wide, None, BF16)
    u = _inproj("glu", xp, xs, w_in_bf, 3 * d_attn, d_conv, tm, tn, row(b_conv_in), F32)

    lam_refs = (row(lq1), row(lk1), row(lq2), row(lk2))
    tq = lambda seq: _tile(seq, min(ATTN_QUERY_TILE, ATTN_SCORE_BYTES // (seq * 12)))
    o_p = _attention(qk, v, lam_refs, row(subln_g), 0, bp, sp, n_heads, lam_init, tq(sp))
    o_s = _attention(qk, v, lam_refs, row(subln_g), tp, bs, ss, n_heads, lam_init, tq(ss))
    conv_args = (conv_w, row(conv_b), row(conv_ln_g), row(conv_ln_b))
    u_p = _conv(u, *conv_args, 0, bp, sp, _tile(math.gcd(sp, tp), 128))
    u_s = _conv(u, *conv_args, tp, bs, ss, _tile(math.gcd(ss, tp), 128))

    rw_f32 = jnp.zeros((d, LANES), F32).at[:, :n_exp].set(router_w)
    rw_hi = rw_f32.astype(BF16)
    rw_pad = jnp.concatenate([rw_hi, (rw_f32 - rw_hi.astype(F32)).astype(BF16)], axis=1)
    rb_pad = jnp.zeros((1, LANES), F32).at[0, :n_exp].set(router_b)
    h, top_idx, gates, rank, counts = _outproj(
        o_p, o_s, u_p, u_s, w_out.astype(BF16), xp, xs, row(ln1_g), row(ln1_b), rw_pad, rb_pad, n_exp, alpha,
        tm, _tile(d_attn, 512))

    bm = _tile(t_all, 512)
    n_assign = t_all * TOP_K
    nb = n_assign // bm + n_exp
    cnt = counts[0, :n_exp]
    blocks = (cnt + bm - 1) // bm
    blocks_end = jnp.cumsum(blocks)
    first_slot = (blocks_end - blocks) * bm
    flat_e = top_idx[:, :TOP_K].reshape(-1)
    pos = jnp.take(first_slot, flat_e) + rank[:, :TOP_K].reshape(-1)
    slot_tok = jnp.zeros((nb * bm,), I32).at[pos].set(jnp.arange(n_assign, dtype=I32) // TOP_K, unique_indices=True)
    block_ids = jnp.arange(nb, dtype=I32)
    block_e = jnp.minimum(jnp.sum((block_ids[:, None] >= blocks_end[None, :]).astype(I32), axis=1), n_exp - 1)
    block_rows = jnp.take(cnt, block_e) - (block_ids - jnp.take(blocks_end - blocks, block_e)) * bm
    block_rows = jnp.where(block_ids < blocks_end[-1], jnp.clip(block_rows, 0, bm), 0).astype(I32)

    tf = _tile(w_dn.shape[1], 1024)
    hmid = _gate_up(block_e, block_rows, slot_tok.reshape(nb, 1, bm), h, w_gu.astype(BF16),
                    b_gu.reshape(n_exp, 1, -1), bm, tf)
    ys = _down(block_e, block_rows, hmid, w_dn.astype(BF16), b_dn.reshape(n_exp, 1, -1), bm, _tile(d, 2048))

    tc = _tile(math.gcd(tp, ts_), 256)
    pos_tiles = pos.reshape(t_all // tc, tc, TOP_K).transpose(0, 2, 1).reshape(t_all // tc, 1, TOP_K * tc)
    y_p = _combine(pos_tiles, gates, h, ys, row(ln2_g), row(ln2_b), 0, tp, alpha, tc)
    y_s = _combine(pos_tiles, gates, h, ys, row(ln2_g), row(ln2_b), tp, ts_, alpha, tc)
    return y_p, y_s


def kernel(x_prompt, x_sample, w_in, b_conv_in, conv_w, conv_b, conv_ln_g, conv_ln_b, lambda_q1, lambda_k1,
           lambda_q2, lambda_k2, subln_g, w_out, ln1_g, ln1_b, router_w, router_b, w_gate_up, b_gate_up, w_down,
           b_down, ln2_g, ln2_b):
    bp, sp, d = x_prompt.shape
    bs, ss, _ = x_sample.shape
    depth = w_in.shape[0]
    alpha = (2.0 * depth) ** 0.25
    xp = x_prompt.reshape(bp * sp, d)
    xs = x_sample.reshape(bs * ss, d)
    for l in range(depth):
        lam_init = 0.8 - 0.6 * math.exp(-0.3 * l)
        xp, xs = _layer(xp, xs, (bp, sp), (bs, ss), lam_init, alpha, w_in[l], b_conv_in[l], conv_w[l], conv_b[l],
                        conv_ln_g[l], conv_ln_b[l], lambda_q1[l], lambda_k1[l], lambda_q2[l], lambda_k2[l],
                        subln_g[l], w_out[l], ln1_g[l], ln1_b[l], router_w[l], router_b[l], w_gate_up[l],
                        b_gate_up[l], w_down[l], b_down[l], ln2_g[l], ln2_b[l])
    return xp.reshape(bp, sp, d), xs.reshape(bs, ss, d)
```

```python
import functools
import math

import jax
import jax.numpy as jnp
from jax import lax
from jax.experimental import pallas as pl
from jax.experimental.pallas import tpu as pltpu

F32 = jnp.float32
BF16 = jnp.bfloat16
I32 = jnp.int32

LANES = 128
SUBLANES = 8
HALO = 16
HEAD_DIM = 128
HEAD_W = 2 * HEAD_DIM
ROT_DIM = HEAD_DIM // 4
ROT_HALF = ROT_DIM // 2
ROPE_THETA = 500000.0
CONV_SIZE = 31
CONV_PAD = (CONV_SIZE - 1) // 2
TOP_K = 4
SWIGLU_LIMIT = 7.0
SWIGLU_ALPHA = 1.702
LN_EPS = 1e-5
LOG2E = 1.4426950408889634
VMEM_LIMIT = 56 * 1024 * 1024
ATTN_QUERY_TILE = 512
ATTN_SCORE_BYTES = 24 * 1024 * 1024
ATTN_KEY_CHUNK = 512


def _tile(dim, pref):
    t = pref
    while t > 1 and dim % t:
        t //= 2
    return t


def _params(sem, vmem=VMEM_LIMIT):
    return pltpu.CompilerParams(dimension_semantics=sem, vmem_limit_bytes=vmem)


def _layer_norm(z, g, b):
    mu = jnp.mean(z, axis=-1, keepdims=True)
    zc = z - mu
    var = jnp.mean(zc * zc, axis=-1, keepdims=True)
    return zc * lax.rsqrt(var + LN_EPS) * g + b


def _row_copy(i, n_p, tm, xp_hbm, xs_hbm, dst, sem):
    cp_p = pltpu.make_async_copy(xp_hbm.at[pl.ds(jnp.minimum(i, n_p - 1) * tm, tm), :], dst, sem)
    cp_s = pltpu.make_async_copy(xs_hbm.at[pl.ds(jnp.maximum(i - n_p, 0) * tm, tm), :], dst, sem)
    return cp_p, cp_s


def _row_start(i, n_p, tm, xp_hbm, xs_hbm, dst, sem):
    cp_p, cp_s = _row_copy(i, n_p, tm, xp_hbm, xs_hbm, dst, sem)
    pl.when(i < n_p)(cp_p.start)
    pl.when(i >= n_p)(cp_s.start)


def _row_wait(i, n_p, tm, xp_hbm, xs_hbm, dst, sem):
    cp_p, cp_s = _row_copy(i, n_p, tm, xp_hbm, xs_hbm, dst, sem)
    pl.when(i < n_p)(cp_p.wait)
    pl.when(i >= n_p)(cp_s.wait)


def _inproj_kernel(mode, n_p, tm, tn, xp_hbm, xs_hbm, *refs):
    if mode == "qk":
        w_ref, cos_ref, sin_ref, o_ref, xf_ref, xb_ref, sem = refs
    elif mode == "v":
        w_ref, o_ref, xf_ref, xb_ref, sem = refs
    else:
        wa_ref, wg_ref, ba_ref, bg_ref, o_ref, xf_ref, xb_ref, sem = refs
    i = pl.program_id(0)
    j = pl.program_id(1)

    @pl.when(j == 0)
    def _load_rows():
        slot = i % 2

        @pl.when(i == 0)
        def _first():
            _row_start(i, n_p, tm, xp_hbm, xs_hbm, xf_ref.at[0], sem.at[0])

        _row_wait(i, n_p, tm, xp_hbm, xs_hbm, xf_ref.at[slot], sem.at[slot])

        @pl.when(i + 1 < pl.num_programs(0))
        def _next():
            _row_start(i + 1, n_p, tm, xp_hbm, xs_hbm, xf_ref.at[1 - slot], sem.at[1 - slot])

        xb_ref[...] = xf_ref[slot].astype(BF16)

    xb = xb_ref[...]
    if mode == "qk":
        acc = jnp.dot(xb, w_ref[...], preferred_element_type=F32)
        c = cos_ref[...]
        s = sin_ref[...]
        lane = lax.broadcasted_iota(I32, (tm, LANES), 1)
        for ch in range(tn // LANES):
            t = acc[:, ch * LANES:(ch + 1) * LANES]
            partner = jnp.where(lane < ROT_HALF, pltpu.roll(t, LANES - ROT_HALF, 1), pltpu.roll(t, ROT_HALF, 1))
            o_ref[:, ch * LANES:(ch + 1) * LANES] = (t * c + partner * s).astype(BF16)
    elif mode == "v":
        o_ref[...] = jnp.dot(xb, w_ref[...], preferred_element_type=F32).astype(BF16)
    else:
        a = jnp.dot(xb, wa_ref[...], preferred_element_type=F32) + ba_ref[...]
        g = jnp.dot(xb, wg_ref[...], preferred_element_type=F32) + bg_ref[...]
        o_ref[...] = a * (1.0 / (1.0 + jnp.exp(-g)))


def _inproj(mode, xp, xs, w_bf, col0, ncols, tm, tn, extra, out_dtype):
    tp, d = xp.shape
    t_all = tp + xs.shape[0]
    n_p = tp // tm
    cb0 = col0 // tn
    any_spec = pl.BlockSpec(memory_space=pl.ANY)
    w_spec = pl.BlockSpec((d, tn), lambda i, j: (0, cb0 + j))
    if mode == "qk":
        cos_t, sin_t, sp_t, ss_t = extra
        pos = lambda i, j: (jnp.where(i < n_p, i % sp_t, (i - n_p) % ss_t), 0)
        in_specs = [any_spec, any_spec, w_spec, pl.BlockSpec((tm, LANES), pos), pl.BlockSpec((tm, LANES), pos)]
        args = (xp, xs, w_bf, cos_t, sin_t)
    elif mode == "v":
        in_specs = [any_spec, any_spec, w_spec]
        args = (xp, xs, w_bf)
    else:
        bias = extra
        gb0 = (col0 + ncols) // tn
        in_specs = [any_spec, any_spec, w_spec, pl.BlockSpec((d, tn), lambda i, j: (0, gb0 + j)),
                    pl.BlockSpec((1, tn), lambda i, j: (0, j)),
                    pl.BlockSpec((1, tn), lambda i, j: (0, ncols // tn + j))]
        args = (xp, xs, w_bf, w_bf, bias, bias)
    return pl.pallas_call(
        functools.partial(_inproj_kernel, mode, n_p, tm, tn),
        out_shape=jax.ShapeDtypeStruct((t_all, ncols), out_dtype),
        grid=(t_all // tm, ncols // tn),
        in_specs=in_specs,
        out_specs=pl.BlockSpec((tm, tn), lambda i, j: (i, j)),
        scratch_shapes=[pltpu.VMEM((2, tm, d), F32), pltpu.VMEM((tm, d), BF16), pltpu.SemaphoreType.DMA((2,))],
        compiler_params=_params(("arbitrary", "arbitrary")),
        name="inproj_" + mode,
    )(*args)


def _lane_fold(x, op):
    parts = [x[:, t * LANES:(t + 1) * LANES] for t in range(x.shape[1] // LANES)]
    while len(parts) > 1:
        parts = [op(parts[t], parts[t + 1]) for t in range(0, len(parts) - 1, 2)] + parts[len(parts) & ~1:]
    return parts[0]


def _attn_kernel(lam_init, tk, q_ref, k_ref, v_ref, lq1_ref, lk1_ref, lq2_ref, lk2_ref, g_ref, o_ref,
                 s_ref, p_ref, fold_ref, acc_ref):
    tq = q_ref.shape[0]
    c = (HEAD_DIM ** -0.5) * LOG2E
    n_chunks = k_ref.shape[0] // tk
    half = n_chunks // 2

    def scores(mp):
        cols = slice(mp * HEAD_DIM, (mp + 1) * HEAD_DIM)
        q = q_ref[:, cols]
        fold = None
        for ch in range(n_chunks):
            keys = slice(ch * tk, (ch + 1) * tk)
            s = lax.dot_general(q, k_ref[keys, cols], (((1,), (1,)), ((), ())), preferred_element_type=F32)
            s_ref[mp, :, keys] = s
            part = _lane_fold(s, jnp.maximum)
            fold = part if fold is None else jnp.maximum(fold, part)
        fold_ref[mp] = fold

    def probs(mp, lo, hi):
        row_max = jnp.max(fold_ref[mp], axis=-1, keepdims=True)
        fold = None if lo == 0 else fold_ref[2 + mp]
        for ch in range(lo, hi):
            keys = slice(ch * tk, (ch + 1) * tk)
            p = jnp.exp2((s_ref[mp, :, keys] - row_max) * c)
            part = _lane_fold(p, jnp.add)
            fold = part if fold is None else fold + part
            p_ref[mp * tq:(mp + 1) * tq, keys] = p.astype(BF16)
        fold_ref[2 + mp] = fold

    def values(lo, hi):
        acc = None
        for ch in range(lo, hi):
            keys = slice(ch * tk, (ch + 1) * tk)
            pv = jnp.dot(p_ref[:, keys], v_ref[keys, :], preferred_element_type=F32)
            acc = pv if acc is None else acc + pv
        return acc

    scores(0)
    scores(1)
    probs(0, 0, n_chunks)
    probs(1, 0, half)
    acc_ref[...] = values(0, half)
    probs(1, half, n_chunks)
    acc = values(half, n_chunks) + acc_ref[...]
    l0 = jnp.sum(fold_ref[2], axis=-1, keepdims=True)
    l1 = jnp.sum(fold_ref[3], axis=-1, keepdims=True)
    lam = (jnp.exp(jnp.sum(lq1_ref[...] * lk1_ref[...], axis=-1, keepdims=True))
           - jnp.exp(jnp.sum(lq2_ref[...] * lk2_ref[...], axis=-1, keepdims=True)) + lam_init)
    o = acc[:tq] * (1.0 / l0) - acc[tq:] * (lam / l1)
    ms = jnp.mean(o * o, axis=-1, keepdims=True)
    o_ref[...] = (o * lax.rsqrt(ms + LN_EPS) * g_ref[...] * (1.0 - lam_init)).astype(BF16)


def _attention(qk, v, lam_refs, subln_g, row0, batch, seq, n_heads, lam_init, tq):
    assert row0 % seq == 0 and seq % tq == 0
    qt = seq // tq
    kcol0 = n_heads
    small = pl.BlockSpec((1, HEAD_DIM), lambda b, h, i: (0, 0))
    return pl.pallas_call(
        functools.partial(_attn_kernel, lam_init, _tile(seq // 2, ATTN_KEY_CHUNK)),
        out_shape=jax.ShapeDtypeStruct((batch * seq, n_heads * HEAD_W), BF16),
        grid=(batch, n_heads, qt),
        in_specs=[pl.BlockSpec((tq, HEAD_W), lambda b, h, i: (row0 // tq + b * qt + i, h)),
                  pl.BlockSpec((seq, HEAD_W), lambda b, h, i: (row0 // seq + b, kcol0 + h)),
                  pl.BlockSpec((seq, HEAD_W), lambda b, h, i: (row0 // seq + b, h)),
                  small, small, small, small,
                  pl.BlockSpec((1, HEAD_W), lambda b, h, i: (0, 0))],
        out_specs=pl.BlockSpec((tq, HEAD_W), lambda b, h, i: (b * qt + i, h)),
        scratch_shapes=[pltpu.VMEM((2, tq, seq), F32), pltpu.VMEM((2 * tq, seq), BF16),
                        pltpu.VMEM((4, tq, LANES), F32), pltpu.VMEM((2 * tq, HEAD_W), F32)],
        compiler_params=_params(("arbitrary", "arbitrary", "arbitrary")),
        name="diff_attention",
    )(qk, qk, v, *lam_refs, subln_g)


def _conv_kernel(ts, nt, rc, prev_ref, x_ref, next_ref, w_ref, cb_ref, g_ref, b_ref, o_ref, pad_ref, sh_ref, u_ref):
    t = pl.program_id(1)
    c_all = x_ref.shape[1]
    pad_ref[0:HALO, :] = jnp.where(t > 0, prev_ref[...], 0.0)
    pad_ref[HALO:HALO + ts, :] = x_ref[...]
    pad_ref[HALO + ts:, :] = jnp.where(t < nt - 1, next_ref[...], 0.0)
    span = sh_ref.shape[1]
    for s in range(1, SUBLANES):
        sh_ref[s - 1] = pad_ref[s:s + span, :]
    off = HALO - CONV_PAD
    for cc in range(c_all // LANES):
        cs = slice(cc * LANES, (cc + 1) * LANES)
        for r in range(ts // rc):
            acc = jnp.zeros((rc, LANES), F32)
            for k in range(CONV_SIZE):
                up, s = divmod(k + off, SUBLANES)
                lo = r * rc + up * SUBLANES
                rows = pad_ref[lo:lo + rc, cs] if s == 0 else sh_ref[s - 1, lo:lo + rc, cs]
                acc = acc + rows * w_ref[k:k + 1, cs]
            u_ref[r * rc:(r + 1) * rc, cs] = acc
    u = _layer_norm(u_ref[...] + cb_ref[...], g_ref[...], b_ref[...])
    o_ref[...] = (u * (1.0 / (1.0 + jnp.exp(-u)))).astype(BF16)


def _conv(u, conv_w, conv_b, ln_g, ln_b, row0, batch, seq, ts):
    c = u.shape[1]
    nt = seq // ts
    assert row0 % ts == 0 and seq % ts == 0 and ts % HALO == 0
    hb = ts // HALO
    last_h = u.shape[0] // HALO - 1
    base = lambda b, t: row0 // ts + b * nt + t
    vec = pl.BlockSpec((1, c), lambda b, t: (0, 0))
    return pl.pallas_call(
        functools.partial(_conv_kernel, ts, nt, _tile(ts, 32)),
        out_shape=jax.ShapeDtypeStruct((batch * seq, c), BF16),
        grid=(batch, nt),
        in_specs=[pl.BlockSpec((HALO, c), lambda b, t: (jnp.maximum(base(b, t) * hb - 1, 0), 0)),
                  pl.BlockSpec((ts, c), lambda b, t: (base(b, t), 0)),
                  pl.BlockSpec((HALO, c), lambda b, t: (jnp.minimum((base(b, t) + 1) * hb, last_h), 0)),
                  pl.BlockSpec((CONV_SIZE, c), lambda b, t: (0, 0)),
                  vec, vec, vec],
        out_specs=pl.BlockSpec((ts, c), lambda b, t: (b * nt + t, 0)),
        scratch_shapes=[pltpu.VMEM((ts + 2 * HALO, c), F32),
                        pltpu.VMEM((SUBLANES - 1, ts + 2 * HALO - SUBLANES, c), F32),
                        pltpu.VMEM((ts, c), F32)],
        compiler_params=_params(("arbitrary", "arbitrary")),
        name="conformer_conv",
    )(u, u, u, conv_w, conv_b, ln_g, ln_b)


def _outproj_kernel(alpha, n_exp, n_p, tm, nk, rc, *refs):
    (op_ref, os_ref, up_ref, us_ref, w_ref, xp_hbm, xs_hbm, g_ref, b_ref, rw_ref, rb_ref,
     h_ref, idx_ref, gate_ref, rank_ref, cnt_ref, acc_ref, x_ref, logit_ref, run_ref, sem) = refs
    i = pl.program_id(0)
    k = pl.program_id(1)
    half = nk // 2

    @pl.when(k == 0)
    def _init():
        acc_ref[...] = jnp.zeros_like(acc_ref)
        _row_start(i, n_p, tm, xp_hbm, xs_hbm, x_ref, sem)

    @pl.when((i == 0) & (k == 0))
    def _init_counts():
        run_ref[...] = jnp.zeros_like(run_ref)

    for src, (lo, is_p) in zip((op_ref, up_ref, os_ref, us_ref), ((0, True), (half, True), (0, False), (half, False))):
        rows = (i < n_p) if is_p else (i >= n_p)

        @pl.when(rows & (k >= lo) & (k < lo + half))
        def _acc(src=src):
            acc_ref[...] += jnp.dot(src[...], w_ref[...], preferred_element_type=F32)

    @pl.when(k == nk - 1)
    def _epilogue():
        _row_wait(i, n_p, tm, xp_hbm, xs_hbm, x_ref, sem)
        for r0 in range(0, tm, rc):
            h = _layer_norm(alpha * x_ref[r0:r0 + rc, :] + acc_ref[r0:r0 + rc, :], g_ref[...], b_ref[...])
            h_ref[r0:r0 + rc, :] = h
            h_hi = h.astype(BF16)
            h_lo = (h - h_hi.astype(F32)).astype(BF16)
            big = jnp.dot(h_hi, rw_ref[...], preferred_element_type=F32)
            small = jnp.dot(h_lo, rw_ref[...], preferred_element_type=F32)
            logit_ref[r0:r0 + rc, :] = big[:, :LANES] + (big[:, LANES:] + small[:, :LANES])
        lane = lax.broadcasted_iota(I32, (tm, LANES), 1)
        lane_f = lane.astype(F32)
        logits = jnp.where(lane < n_exp, logit_ref[...] + rb_ref[...], -jnp.inf)
        idx_out = jnp.zeros((tm, LANES), F32)
        val_out = jnp.zeros((tm, LANES), F32)
        member = jnp.zeros((tm, LANES), F32)
        picks = []
        top = None
        for j in range(TOP_K):
            mx = jnp.max(logits, axis=-1, keepdims=True)
            top = mx if top is None else top
            pick = jnp.min(jnp.where(logits == mx, lane_f, float(LANES)), axis=-1, keepdims=True)
            hit = lane_f == pick
            picks.append(hit)
            idx_out = jnp.where(lane == j, pick, idx_out)
            val_out = jnp.where(lane == j, mx, val_out)
            member = jnp.where(hit, 1.0, member)
            logits = jnp.where(hit, -jnp.inf, logits)
        e = jnp.where(lane < TOP_K, jnp.exp(val_out - top), 0.0)
        gate_ref[...] = e * (1.0 / jnp.sum(e, axis=-1, keepdims=True))
        idx_ref[...] = idx_out.astype(I32)
        row = lax.broadcasted_iota(I32, (tm, tm), 0)
        col = lax.broadcasted_iota(I32, (tm, tm), 1)
        before = jnp.where(col < row, 1.0, 0.0).astype(BF16)
        prior = jnp.dot(before, member.astype(BF16), preferred_element_type=F32) + run_ref[...]
        rank_out = jnp.zeros((tm, LANES), F32)
        for j in range(TOP_K):
            rj = jnp.sum(jnp.where(picks[j], prior, 0.0), axis=-1, keepdims=True)
            rank_out = jnp.where(lane == j, rj, rank_out)
        rank_ref[...] = rank_out.astype(I32)
        run_ref[...] += jnp.sum(member, axis=0, keepdims=True)
        cnt_ref[...] = run_ref[...].astype(I32)


def _outproj(o_p, o_s, u_p, u_s, w_out_bf, xp, xs, ln_g, ln_b, rw_pad, rb_pad, n_exp, alpha, tm, tk):
    tp, d = xp.shape
    t_all = tp + xs.shape[0]
    n_p = tp // tm
    n_s = xs.shape[0] // tm
    half = o_p.shape[1] // tk
    nk = 2 * half
    rowp = lambda i: jnp.minimum(i, n_p - 1)
    rows = lambda i: jnp.clip(i - n_p, 0, n_s - 1)
    attn_k = lambda k: jnp.minimum(k, half - 1)
    conv_k = lambda k: jnp.clip(k - half, 0, half - 1)
    any_spec = pl.BlockSpec(memory_space=pl.ANY)
    vec = pl.BlockSpec((1, d), lambda i, k: (0, 0))
    tok = pl.BlockSpec((tm, LANES), lambda i, k: (i, 0))
    return pl.pallas_call(
        functools.partial(_outproj_kernel, alpha, n_exp, n_p, tm, nk, _tile(tm, 128)),
        out_shape=(jax.ShapeDtypeStruct((t_all, d), F32),
                   jax.ShapeDtypeStruct((t_all, LANES), I32),
                   jax.ShapeDtypeStruct((t_all, LANES), F32),
                   jax.ShapeDtypeStruct((t_all, LANES), I32),
                   jax.ShapeDtypeStruct((1, LANES), I32)),
        grid=(t_all // tm, nk),
        in_specs=[pl.BlockSpec((tm, tk), lambda i, k: (rowp(i), attn_k(k))),
                  pl.BlockSpec((tm, tk), lambda i, k: (rows(i), attn_k(k))),
                  pl.BlockSpec((tm, tk), lambda i, k: (rowp(i), conv_k(k))),
                  pl.BlockSpec((tm, tk), lambda i, k: (rows(i), conv_k(k))),
                  pl.BlockSpec((tk, d), lambda i, k: (k, 0)),
                  any_spec, any_spec, vec, vec,
                  pl.BlockSpec((d, 2 * LANES), lambda i, k: (0, 0)),
                  pl.BlockSpec((1, LANES), lambda i, k: (0, 0))],
        out_specs=(pl.BlockSpec((tm, d), lambda i, k: (i, 0)), tok, tok, tok,
                   pl.BlockSpec((1, LANES), lambda i, k: (0, 0))),
        scratch_shapes=[pltpu.VMEM((tm, d), F32), pltpu.VMEM((tm, d), F32), pltpu.VMEM((tm, LANES), F32),
                        pltpu.VMEM((1, LANES), F32),
                        pltpu.SemaphoreType.DMA(())],
        compiler_params=_params(("arbitrary", "arbitrary")),
        name="outproj_ln_router",
    )(o_p, o_s, u_p, u_s, w_out_bf, xp, xs, ln_g, ln_b, rw_pad, rb_pad)


def _row_gather_start(tok_ref, r, src_hbm, dst_ref, sem):
    pltpu.make_async_copy(src_hbm.at[pl.ds(tok_ref[r], 1), :], dst_ref.at[pl.ds(r, 1), :], sem).start()


def _row_gather_wait(n, src_hbm, dst_ref, sem):
    pltpu.make_async_copy(src_hbm.at[pl.ds(0, n), :], dst_ref.at[pl.ds(0, n), :], sem).wait()


def _gate_up_kernel(bm, nj, be_ref, rows_ref, tok_ref, nxt_ref, h_hbm, wg_ref, wu_ref, bg_ref, bu_ref, o_ref,
                    xf_ref, xb_ref, sem):
    m = pl.program_id(0)
    j = pl.program_id(1)
    rows = rows_ref[m]
    active = rows > 0
    step_rows = bm // nj
    half = bm // 2

    @pl.when((m == 0) & (j == 0))
    def _first_block():
        def start(r, carry):
            _row_gather_start(tok_ref.at[0, 0], r, h_hbm, xf_ref, sem)
            return carry
        lax.fori_loop(0, bm, start, 0, unroll=8)

    @pl.when(j == 0)
    def _rows_ready():
        prev_active = rows_ref[jnp.maximum(m - 1, 0)] > 0

        @pl.when(active | prev_active)
        def _wait():
            _row_gather_wait(bm, h_hbm, xf_ref, sem)

        @pl.when(active)
        def _cast():
            xb_ref[...] = xf_ref[...].astype(BF16)

    def compute(n):
        for r in range(step_rows):
            _row_gather_start(nxt_ref.at[0, 0], j * step_rows + r, h_hbm, xf_ref, sem)
        xb = xb_ref[0:n, :]
        g = jnp.dot(xb, wg_ref[...], preferred_element_type=F32) + bg_ref[...]
        up = jnp.dot(xb, wu_ref[...], preferred_element_type=F32) + bu_ref[...]
        g = jnp.minimum(g, SWIGLU_LIMIT)
        up = jnp.clip(up, -SWIGLU_LIMIT, SWIGLU_LIMIT)
        o_ref[0:n, :] = ((up + 1.0) * (g * (1.0 / (1.0 + jnp.exp(-SWIGLU_ALPHA * g))))).astype(BF16)
        if n < bm:
            o_ref[n:, :] = jnp.zeros((bm - n, o_ref.shape[1]), BF16)

    pl.when(rows > half)(lambda: compute(bm))
    pl.when(active & (rows <= half))(lambda: compute(half))

    @pl.when(jnp.logical_not(active))
    def _idle():
        o_ref[...] = jnp.zeros_like(o_ref)


def _gate_up(block_e, block_rows, slot_tok, h, w_gu_bf, b_gu, bm, tn):
    nb = block_e.shape[0]
    d = h.shape[1]
    f = w_gu_bf.shape[2] // 2
    nj = f // tn
    col = lambda j, act, m: jnp.where(act[m] > 0, j, nj - 1)
    grid_spec = pltpu.PrefetchScalarGridSpec(
        num_scalar_prefetch=2,
        grid=(nb, nj),
        in_specs=[pl.BlockSpec((1, 1, bm), lambda m, j, be, act: (m, 0, 0), memory_space=pltpu.SMEM),
                  pl.BlockSpec((1, 1, bm), lambda m, j, be, act: (jnp.minimum(m + 1, nb - 1), 0, 0),
                               memory_space=pltpu.SMEM),
                  pl.BlockSpec(memory_space=pl.ANY),
                  pl.BlockSpec((None, d, tn), lambda m, j, be, act: (be[m], 0, col(j, act, m))),
                  pl.BlockSpec((None, d, tn), lambda m, j, be, act: (be[m], 0, nj + col(j, act, m))),
                  pl.BlockSpec((None, 1, tn), lambda m, j, be, act: (be[m], 0, col(j, act, m))),
                  pl.BlockSpec((None, 1, tn), lambda m, j, be, act: (be[m], 0, nj + col(j, act, m)))],
        out_specs=pl.BlockSpec((bm, tn), lambda m, j, be, act: (m, j)),
        scratch_shapes=[pltpu.VMEM((bm, d), F32), pltpu.VMEM((bm, d), BF16), pltpu.SemaphoreType.DMA(())],
    )
    return pl.pallas_call(
        functools.partial(_gate_up_kernel, bm, nj),
        out_shape=jax.ShapeDtypeStruct((nb * bm, f), BF16),
        grid_spec=grid_spec,
        compiler_params=_params(("arbitrary", "arbitrary")),
        name="moe_gate_up",
    )(block_e, block_rows, slot_tok, slot_tok, h, w_gu_bf, w_gu_bf, b_gu, b_gu)


def _down_kernel(be_ref, rows_ref, x_ref, w_ref, b_ref, o_ref):
    rows = rows_ref[pl.program_id(0)]
    active = rows > 0
    bm = x_ref.shape[0]
    half = bm // 2

    def compute(n):
        o_ref[0:n, :] = jnp.dot(x_ref[0:n, :], w_ref[...], preferred_element_type=F32) + b_ref[...]
        if n < bm:
            o_ref[n:, :] = jnp.zeros((bm - n, o_ref.shape[1]), F32)

    pl.when(rows > half)(lambda: compute(bm))
    pl.when(active & (rows <= half))(lambda: compute(half))

    @pl.when(jnp.logical_not(active))
    def _idle():
        o_ref[...] = jnp.zeros_like(o_ref)


def _down(block_e, block_rows, hmid, w_dn_bf, b_dn, bm, tn):
    nb = block_e.shape[0]
    f = hmid.shape[1]
    d = w_dn_bf.shape[2]
    nj = d // tn
    col = lambda j, act, m: jnp.where(act[m] > 0, j, nj - 1)
    grid_spec = pltpu.PrefetchScalarGridSpec(
        num_scalar_prefetch=2,
        grid=(nb, nj),
        in_specs=[pl.BlockSpec((bm, f), lambda m, j, be, act: (m, 0)),
                  pl.BlockSpec((None, f, tn), lambda m, j, be, act: (be[m], 0, col(j, act, m))),
                  pl.BlockSpec((None, 1, tn), lambda m, j, be, act: (be[m], 0, col(j, act, m)))],
        out_specs=pl.BlockSpec((bm, tn), lambda m, j, be, act: (m, j)),
    )
    return pl.pallas_call(
        _down_kernel,
        out_shape=jax.ShapeDtypeStruct((nb * bm, d), F32),
        grid_spec=grid_spec,
        compiler_params=_params(("arbitrary", "arbitrary")),
        name="moe_down",
    )(block_e, block_rows, hmid, w_dn_bf, b_dn)


def _combine_kernel(alpha, tc, pos_ref, nxt_ref, gate_ref, h_ref, ys_hbm, g_ref, b_ref, o_ref, buf_ref, sem):
    i = pl.program_id(0)
    slot = i % 2
    n_rows = TOP_K * tc

    def start_rows(tile_pos_ref, to):
        def start(pair, carry):
            for lane in range(2):
                r = 2 * pair + lane
                pltpu.make_async_copy(ys_hbm.at[pl.ds(tile_pos_ref[0, 0, r], 1), :],
                                      buf_ref.at[to, pl.ds(r, 1), :], sem.at[to]).start(priority=lane)
            return carry
        lax.fori_loop(0, n_rows // 2, start, 0, unroll=4)

    pl.when(i == 0)(lambda: start_rows(pos_ref, 0))
    pl.when(i + 1 < pl.num_programs(0))(lambda: start_rows(nxt_ref, 1 - slot))
    _row_gather_wait(n_rows, ys_hbm, buf_ref.at[slot], sem.at[slot])
    gates = gate_ref[...]
    f = jnp.zeros(h_ref.shape, F32)
    for j in range(TOP_K):
        f = f + gates[:, j:j + 1] * buf_ref[slot, j * tc:(j + 1) * tc, :]
    o_ref[...] = _layer_norm(alpha * h_ref[...] + f, g_ref[...], b_ref[...])


def _combine(pos_tiles, gates, h, ys, ln_g, ln_b, row0, n_rows, alpha, tc):
    d = h.shape[1]
    t0 = row0 // tc
    vec = pl.BlockSpec((1, d), lambda i: (0, 0))
    return pl.pallas_call(
        functools.partial(_combine_kernel, alpha, tc),
        out_shape=jax.ShapeDtypeStruct((n_rows, d), F32),
        grid=(n_rows // tc,),
        in_specs=[pl.BlockSpec((1, 1, TOP_K * tc), lambda i: (t0 + i, 0, 0), memory_space=pltpu.SMEM),
                  pl.BlockSpec((1, 1, TOP_K * tc), lambda i: (t0 + jnp.minimum(i + 1, n_rows // tc - 1), 0, 0),
                               memory_space=pltpu.SMEM),
                  pl.BlockSpec((tc, LANES), lambda i: (t0 + i, 0)),
                  pl.BlockSpec((tc, d), lambda i: (t0 + i, 0)),
                  pl.BlockSpec(memory_space=pl.ANY), vec, vec],
        out_specs=pl.BlockSpec((tc, d), lambda i: (i, 0)),
        scratch_shapes=[pltpu.VMEM((2, TOP_K * tc, d), F32), pltpu.SemaphoreType.DMA((2,))],
        compiler_params=_params(("arbitrary",)),
        name="moe_combine_ln",
    )(pos_tiles, pos_tiles, gates, h, ys, ln_g, ln_b)


def _rope_tables(seq_len):
    inv_freq = ROPE_THETA ** (-jnp.arange(0, ROT_DIM, 2, dtype=F32) / ROT_DIM)
    ang = jnp.arange(seq_len, dtype=F32)[:, None] * inv_freq[None, :]
    cos, sin = jnp.cos(ang), jnp.sin(ang)
    rest = HEAD_DIM - ROT_DIM
    cos_t = jnp.concatenate([cos, cos, jnp.ones((seq_len, rest), F32)], axis=1)
    sin_t = jnp.concatenate([-sin, sin, jnp.zeros((seq_len, rest), F32)], axis=1)
    return cos_t, sin_t


def _layer(xp, xs, shape_p, shape_s, lam_init, alpha, w_in, b_conv_in, conv_w, conv_b, conv_ln_g, conv_ln_b,
           lq1, lk1, lq2, lk2, subln_g, w_out, ln1_g, ln1_b, router_w, router_b, w_gu, b_gu, w_dn, b_dn, ln2_g, ln2_b):
    (bp, sp), (bs, ss) = shape_p, shape_s
    tp, d = xp.shape
    ts_ = xs.shape[0]
    t_all = tp + ts_
    d_attn = d // 2
    d_conv = d - d_attn
    n_heads = d_attn // HEAD_W
    n_exp = router_w.shape[1]
    row = lambda a: a.reshape(1, -1)

    tm = _tile(math.gcd(math.gcd(tp, ts_), math.gcd(sp, ss)), 512)
    tn = _tile(math.gcd(2 * d_attn, d_conv), 512)
    w_in_bf = w_in.astype(BF16)
    cos_t, sin_t = _rope_tables(max(sp, ss))
    tn_wide = _tile(d_attn, 2 * tn)
    qk = _inproj("qk", xp, xs, w_in_bf, 0, 2 * d_attn, tm, tn_wide, (cos_t, sin_t, sp // tm, ss // tm), BF16)
    v = _inproj("v", xp, xs, w_in_bf, 2 * d_attn, d_attn, tm, tn_wide, None, BF16)
    u = _inproj("glu", xp, xs, w_in_bf, 3 * d_attn, d_conv, tm, tn, row(b_conv_in), F32)

    lam_refs = (row(lq1), row(lk1), row(lq2), row(lk2))
    tq = lambda seq: _tile(seq, min(ATTN_QUERY_TILE, ATTN_SCORE_BYTES // (seq * 12)))
    o_p = _attention(qk, v, lam_refs, row(subln_g), 0, bp, sp, n_heads, lam_init, tq(sp))
    o_s = _attention(qk, v, lam_refs, row(subln_g), tp, bs, ss, n_heads, lam_init, tq(ss))
    conv_args = (conv_w, row(conv_b), row(conv_ln_g), row(conv_ln_b))
    u_p = _conv(u, *conv_args, 0, bp, sp, _tile(math.gcd(sp, tp), 128))
    u_s = _conv(u, *conv_args, tp, bs, ss, _tile(math.gcd(ss, tp), 128))

    rw_f32 = jnp.zeros((d, LANES), F32).at[:, :n_exp].set(router_w)
    rw_hi = rw_f32.astype(BF16)
    rw_pad = jnp.concatenate([rw_hi, (rw_f32 - rw_hi.astype(F32)).astype(BF16)], axis=1)
    rb_pad = jnp.zeros((1, LANES), F32).at[0, :n_exp].set(router_b)
    h, top_idx, gates, rank, counts = _outproj(
        o_p, o_s, u_p, u_s, w_out.astype(BF16), xp, xs, row(ln1_g), row(ln1_b), rw_pad, rb_pad, n_exp, alpha,
        tm, _tile(d_attn, 512))

    bm = _tile(t_all, 512)
    n_assign = t_all * TOP_K
    nb = n_assign // bm + n_exp
    cnt = counts[0, :n_exp]
    blocks = (cnt + bm - 1) // bm
    blocks_end = jnp.cumsum(blocks)
    first_slot = (blocks_end - blocks) * bm
    flat_e = top_idx[:, :TOP_K].reshape(-1)
    pos = jnp.take(first_slot, flat_e) + rank[:, :TOP_K].reshape(-1)
    slot_tok = jnp.zeros((nb * bm,), I32).at[pos].set(jnp.arange(n_assign, dtype=I32) // TOP_K, unique_indices=True)
    block_ids = jnp.arange(nb, dtype=I32)
    block_e = jnp.minimum(jnp.sum((block_ids[:, None] >= blocks_end[None, :]).astype(I32), axis=1), n_exp - 1)
    block_rows = jnp.take(cnt, block_e) - (block_ids - jnp.take(blocks_end - blocks, block_e)) * bm
    block_rows = jnp.where(block_ids < blocks_end[-1], jnp.clip(block_rows, 0, bm), 0).astype(I32)

    tf = _tile(w_dn.shape[1], 1024)
    hmid = _gate_up(block_e, block_rows, slot_tok.reshape(nb, 1, bm), h, w_gu.astype(BF16),
                    b_gu.reshape(n_exp, 1, -1), bm, tf)
    ys = _down(block_e, block_rows, hmid, w_dn.astype(BF16), b_dn.reshape(n_exp, 1, -1), bm, _tile(d, 2048))

    tc = _tile(math.gcd(tp, ts_), 256)
    pos_tiles = pos.reshape(t_all // tc, tc, TOP_K).transpose(0, 2, 1).reshape(t_all // tc, 1, TOP_K * tc)
    y_p = _combine(pos_tiles, gates, h, ys, row(ln2_g), row(ln2_b), 0, tp, alpha, tc)
    y_s = _combine(pos_tiles, gates, h, ys, row(ln2_g), row(ln2_b), tp, ts_, alpha, tc)
    return y_p, y_s


def kernel(x_prompt, x_sample, w_in, b_conv_in, conv_w, conv_b, conv_ln_g, conv_ln_b, lambda_q1, lambda_k1,
           lambda_q2, lambda_k2, subln_g, w_out, ln1_g, ln1_b, router_w, router_b, w_gate_up, b_gate_up, w_down,
           b_down, ln2_g, ln2_b):
    bp, sp, d = x_prompt.shape
    bs, ss, _ = x_sample.shape
    depth = w_in.shape[0]
    alpha = (2.0 * depth) ** 0.25
    xp = x_prompt.reshape(bp * sp, d)
    xs = x_sample.reshape(bs * ss, d)
    for l in range(depth):
        lam_init = 0.8 - 0.6 * math.exp(-0.3 * l)
        xp, xs = _layer(xp, xs, (bp, sp), (bs, ss), lam_init, alpha, w_in[l], b_conv_in[l], conv_w[l], conv_b[l],
                        conv_ln_g[l], conv_ln_b[l], lambda_q1[l], lambda_k1[l], lambda_q2[l], lambda_k2[l],
                        subln_g[l], w_out[l], ln1_g[l], ln1_b[l], router_w[l], router_b[l], w_gate_up[l],
                        b_gate_up[l], w_down[l], b_down[l], ln2_g[l], ln2_b[l])
    return xp.reshape(bp, sp, d), xs.reshape(bs, ss, d)
```
